```python
import math
import jax, jax.numpy as jnp
from jax import lax
import numpy as np

D_MODEL = 2048
BATCH = 2
SEQ = 4096
DEPTH = 4
DEC_BATCH = 8
DEC_SEQ = 4
PAST_LEN = 16384
PAGE_SIZE = 128

RET_HEADS = 8
RET_DK = D_MODEL // 16
RET_DV = 2 * RET_DK
RET_CHUNK = 128
MOBA_HEADS = 8
MOBA_DH = D_MODEL // 16
MOBA_BLOCK = 256
MOBA_TOPK = 3
MOBA_QCHUNK = 32
REL_BUCKETS = 32
REL_MAX_DIST = 128
D_FF = 4 * D_MODEL
ROPE_BASE = 10000.0
NORM_EPS = 1e-6

RET_QK_W = RET_HEADS * RET_DK
RET_V_W = RET_HEADS * RET_DV
MOBA_W = MOBA_HEADS * MOBA_DH
SPLITS = (RET_QK_W, RET_QK_W, RET_V_W, RET_V_W, MOBA_W, MOBA_W, MOBA_W, D_MODEL, D_MODEL)
D_IN = 2 * RET_QK_W + 2 * RET_V_W + 3 * MOBA_W + 2 * D_MODEL

kernel_name = "retention_moba_gated_hybrid_step"

F32 = jnp.float32


def rms_norm(x, w):
    xf = x.astype(F32)
    y = xf * lax.rsqrt(jnp.mean(xf * xf, axis=-1, keepdims=True) + NORM_EPS)
    return (y * w.astype(F32)).astype(x.dtype)


def head_rms(o, w):
    y = o * lax.rsqrt(jnp.mean(o * o, axis=-1, keepdims=True) + NORM_EPS)
    return y * w.astype(F32).reshape(RET_HEADS, RET_DV)


def rope(x, pos):
    half = x.shape[-1] // 2
    inv = ROPE_BASE ** (-jnp.arange(half, dtype=F32) / half)
    ang = pos.astype(F32)[:, None] * inv[None, :]
    cos = jnp.cos(ang)[None, :, None, :]
    sin = jnp.sin(ang)[None, :, None, :]
    x1, x2 = x[..., :half], x[..., half:]
    return jnp.concatenate([x1 * cos - x2 * sin, x2 * cos + x1 * sin], axis=-1)


def split_cols(z):
    idx, acc = [], 0
    for w in SPLITS[:-1]:
        acc += w
        idx.append(acc)
    return jnp.split(z, idx, axis=-1)


def ret_log_gamma():
    return jnp.log(1.0 - 2.0 ** (-5.0 - jnp.arange(RET_HEADS, dtype=F32)))


def retention_chunk(q, k, v, s, log_g):
    C = q.shape[1]
    i = jnp.arange(C, dtype=F32)
    diff = i[:, None] - i[None, :]
    causal = diff >= 0
    dmask = jnp.where(causal[None], jnp.exp(log_g[:, None, None] * jnp.where(causal, diff, 0.0)[None]), 0.0)
    scores = jnp.einsum('nihd,njhd->nhij', q, k) * dmask[None]
    o_inner = jnp.einsum('nhij,njhe->nihe', scores, v)
    q_dec = jnp.exp(log_g[None, :] * (i[:, None] + 1.0))
    o_cross = jnp.einsum('nihd,nhde->nihe', q, s) * q_dec[None, :, :, None]
    k_dec = jnp.exp(log_g[None, :] * (C - 1.0 - i)[:, None])
    s_new = jnp.exp(log_g * C)[None, :, None, None] * s + jnp.einsum('njhd,njhe->nhde', k * k_dec[None, :, :, None], v)
    return o_inner + o_cross, s_new


def retention_prompt(q, k, v, log_g):
    N, T, H, dk = q.shape
    dv = v.shape[-1]
    C = RET_CHUNK if T % RET_CHUNK == 0 else T
    nc = T // C

    def to_chunks(a):
        return a.reshape(N, nc, C, H, a.shape[-1]).transpose(1, 0, 2, 3, 4)

    def step(s, blk):
        o, s = retention_chunk(blk[0], blk[1], blk[2], s, log_g)
        return s, o

    s0 = jnp.zeros((N, H, dk, dv), F32)
    s_fin, o = lax.scan(step, s0, (to_chunks(q), to_chunks(k), to_chunks(v)))
    return o.transpose(1, 0, 2, 3, 4).reshape(N, T, H, dv), s_fin


def t5_bucket(dist):
    n = jnp.maximum(dist, 0)
    max_exact = REL_BUCKETS // 2
    nf = jnp.maximum(n, 1).astype(F32)
    large = max_exact + (jnp.log(nf / max_exact) / math.log(REL_MAX_DIST / max_exact)
                         * (REL_BUCKETS - max_exact)).astype(jnp.int32)
    large = jnp.minimum(large, REL_BUCKETS - 1)
    return jnp.where(n < max_exact, n, large)


def moba_block(q, q_pos, k_mean, k_src, v_src, rows, rel_bias):
    N, T, H, dh = q.shape
    NB = k_mean.shape[1]
    own = q_pos // MOBA_BLOCK
    gate = jnp.einsum('nthd,nbhd->nhtb', q, k_mean)
    fully_past = jnp.arange(NB)[None, :] < own[:, None]
    gate = jnp.where(fully_past[None, None], gate, -jnp.inf)
    _, top = lax.top_k(gate, MOBA_TOPK)
    own_b = jnp.broadcast_to(own[None, None, :, None], (N, H, T, 1))
    sel = jnp.concatenate([top, own_b], axis=-1)
    blk_ok = jnp.concatenate([top < own[None, None, :, None], jnp.ones((N, H, T, 1), bool)], axis=-1)
    kpos = sel[..., None] * MOBA_BLOCK + jnp.arange(MOBA_BLOCK)
    qp = q_pos[None, None, :, None, None]
    ok = blk_ok[..., None] & (kpos <= qp)
    row = jax.vmap(lambda r, p: r[p])(rows, kpos)
    hidx = jnp.arange(H)[None, :, None, None, None]
    kg = k_src[row, hidx].astype(F32)
    vg = v_src[row, hidx].astype(F32)
    logits = jnp.einsum('nthd,nhtsjd->nhtsj', q, kg)
    bias = rel_bias.astype(F32)[t5_bucket(qp - kpos), hidx]
    logits = jnp.where(ok, logits + bias, -jnp.inf)
    S = MOBA_TOPK + 1
    p = jax.nn.softmax(logits.reshape(N, H, T, S * MOBA_BLOCK), axis=-1)
    return jnp.einsum('nhtk,nhtkd->nthd', p, vg.reshape(N, H, T, S * MOBA_BLOCK, dh))


def moba_attention(q, q_pos, k_src, v_src, rows, rel_bias):
    N, T, H, dh = q.shape
    nb = rows.shape[1] // MOBA_BLOCK
    k_rows = k_src[rows].astype(F32)
    k_mean = k_rows.reshape(N, nb, MOBA_BLOCK, H, dh).mean(axis=2)
    if T > MOBA_QCHUNK and T % MOBA_QCHUNK == 0:
        nq = T // MOBA_QCHUNK
        qc = q.reshape(N, nq, MOBA_QCHUNK, H, dh).transpose(1, 0, 2, 3, 4)
        pc = q_pos.reshape(nq, MOBA_QCHUNK)
        out = lax.map(lambda a: moba_block(a[0], a[1], k_mean, k_src, v_src, rows, rel_bias), (qc, pc))
        return out.transpose(1, 0, 2, 3, 4).reshape(N, T, H, dh)
    return moba_block(q, q_pos, k_mean, k_src, v_src, rows, rel_bias)


def moba_rows_prompt(n, s):
    nb = max(-(-s // MOBA_BLOCK), MOBA_TOPK)
    l = jnp.arange(nb * MOBA_BLOCK)
    return (jnp.arange(n)[:, None] * s + jnp.minimum(l, s - 1)[None, :]).astype(jnp.int32)


def moba_rows_sample(page_table, page_size, dec_seq, pool_rows):
    n, n_pages = page_table.shape
    past = n_pages * page_size
    total = past + dec_seq
    nb = max(-(-total // MOBA_BLOCK), MOBA_TOPK)
    l = jnp.arange(nb * MOBA_BLOCK)
    lp = jnp.minimum(l, past - 1)
    pool_row = page_table[:, lp // page_size] * page_size + (lp % page_size)[None, :]
    new_row = pool_rows + jnp.arange(n)[:, None] * dec_seq + jnp.clip(l - past, 0, dec_seq - 1)[None, :]
    return jnp.where((l < past)[None, :], pool_row, new_row).astype(jnp.int32)


def mixer_inputs(x, pos, norm_w, w_in):
    N, T, _ = x.shape
    h = rms_norm(x, norm_w)
    qa, ka, va, ga, qb, kb, vb, gate_a, gate_b = split_cols(h @ w_in)
    qa = rope(qa.reshape(N, T, RET_HEADS, RET_DK).astype(F32), pos)
    ka = rope(ka.reshape(N, T, RET_HEADS, RET_DK).astype(F32), pos) * (RET_DK ** -0.5)
    va = va.reshape(N, T, RET_HEADS, RET_DV).astype(F32)
    qb = qb.reshape(N, T, MOBA_HEADS, MOBA_DH).astype(F32) * (MOBA_DH ** -0.5)
    kb = kb.reshape(N, T, MOBA_HEADS, MOBA_DH)
    vb = vb.reshape(N, T, MOBA_HEADS, MOBA_DH)
    return qa, ka, va, ga, qb, kb, vb, gate_a, gate_b


def merge_branches(x, ret_o, ga, moba_o, gate_a, gate_b, ret_norm_w, w_pa, w_pb, w_o):
    N, T, _ = x.shape
    y_a = head_rms(ret_o, ret_norm_w).reshape(N, T, RET_V_W) * jax.nn.silu(ga.astype(F32))
    u_a = y_a.astype(x.dtype) @ w_pa
    u_b = moba_o.reshape(N, T, MOBA_W).astype(x.dtype) @ w_pb
    m = jax.nn.sigmoid(gate_a) * u_a + jax.nn.sigmoid(gate_b) * u_b
    return m @ w_o


def channel_mix(x, norm_w, w1, w2):
    h = rms_norm(x, norm_w)
    return jnp.square(jax.nn.relu(h @ w1)) @ w2


def setup_inputs(seed: int = 0) -> dict:
    key = jax.random.key(seed)
    ks = jax.random.split(key, 17)
    n_pages = PAST_LEN // PAGE_SIZE
    used = DEC_BATCH * n_pages
    n_phys = used + max(1, used // 4)
    nrm = jax.random.normal
    return {
        "x_prompt": nrm(ks[0], (BATCH, SEQ, D_MODEL), F32),
        "x_sample": nrm(ks[1], (DEC_BATCH, DEC_SEQ, D_MODEL), F32),
        "cache_k": nrm(ks[2], (DEPTH, n_phys, PAGE_SIZE, MOBA_HEADS, MOBA_DH), F32),
        "cache_v": nrm(ks[3], (DEPTH, n_phys, PAGE_SIZE, MOBA_HEADS, MOBA_DH), F32),
        "state_ret": 0.5 * nrm(ks[4], (DEPTH, DEC_BATCH, RET_HEADS, RET_DK, RET_DV), F32),
        "page_table": jax.random.permutation(ks[5], n_phys)[:used].reshape(DEC_BATCH, n_pages).astype(jnp.int32),
        "rel_bias": 0.5 * nrm(ks[6], (REL_BUCKETS, MOBA_HEADS), F32),
        "norm1_w": 1.0 + 0.01 * nrm(ks[7], (DEPTH, D_MODEL), F32),
        "w_in": nrm(ks[8], (DEPTH, D_MODEL, D_IN), F32) * D_MODEL ** -0.5,
        "ret_norm_w": 1.0 + 0.01 * nrm(ks[9], (DEPTH, RET_V_W), F32),
        "w_pa": nrm(ks[10], (DEPTH, RET_V_W, D_MODEL), F32) * RET_V_W ** -0.5,
        "w_pb": nrm(ks[11], (DEPTH, MOBA_W, D_MODEL), F32) * MOBA_W ** -0.5,
        "w_o": nrm(ks[12], (DEPTH, D_MODEL, D_MODEL), F32) * D_MODEL ** -0.5,
        "norm2_w": 1.0 + 0.01 * nrm(ks[13], (DEPTH, D_MODEL), F32),
        "w_ff1": nrm(ks[14], (DEPTH, D_MODEL, D_FF), F32) * D_MODEL ** -0.5,
        "w_ff2": nrm(ks[15], (DEPTH, D_FF, D_MODEL), F32) * D_FF ** -0.5,
        "final_norm_w": 1.0 + 0.01 * nrm(ks[16], (D_MODEL,), F32),
    }


def reference(x_prompt, x_sample, cache_k, cache_v, state_ret, page_table, rel_bias, norm1_w, w_in,
              ret_norm_w, w_pa, w_pb, w_o, norm2_w, w_ff1, w_ff2, final_norm_w):
    log_g = ret_log_gamma()

    B, S, _ = x_prompt.shape
    pos_p = jnp.arange(S, dtype=jnp.int32)
    rows_p = moba_rows_prompt(B, S)
    xp = x_prompt
    pk, pv, ps = [], [], []
    for l in range(DEPTH):
        qa, ka, va, ga, qb, kb, vb, gate_a, gate_b = mixer_inputs(xp, pos_p, norm1_w[l], w_in[l])
        ret_o, s_fin = retention_prompt(qa, ka, va, log_g)
        k_src = kb.reshape(B * S, MOBA_HEADS, MOBA_DH)
        v_src = vb.reshape(B * S, MOBA_HEADS, MOBA_DH)
        moba_o = moba_attention(qb, pos_p, k_src, v_src, rows_p, rel_bias)
        xp = xp + merge_branches(xp, ret_o, ga, moba_o, gate_a, gate_b, ret_norm_w[l], w_pa[l], w_pb[l], w_o[l])
        xp = xp + channel_mix(xp, norm2_w[l], w_ff1[l], w_ff2[l])
        pk.append(kb)
        pv.append(vb)
        ps.append(s_fin.astype(state_ret.dtype))
    y_prompt = rms_norm(xp, final_norm_w)

    Ns, Ts, _ = x_sample.shape
    page_size = cache_k.shape[2]
    past_len = page_table.shape[1] * page_size
    pool_rows = cache_k.shape[1] * page_size
    pos_s = past_len + jnp.arange(Ts, dtype=jnp.int32)
    rows_s = moba_rows_sample(page_table, page_size, Ts, pool_rows)
    xs = x_sample
    sk, sv, ss = [], [], []
    for l in range(DEPTH):
        qa, ka, va, ga, qb, kb, vb, gate_a, gate_b = mixer_inputs(xs, pos_s, norm1_w[l], w_in[l])
        ret_o, s_new = retention_chunk(qa, ka, va, state_ret[l].astype(F32), log_g)
        pool_k = cache_k[l].reshape(pool_rows, MOBA_HEADS, MOBA_DH)
        pool_v = cache_v[l].reshape(pool_rows, MOBA_HEADS, MOBA_DH)
        k_src = jnp.concatenate([pool_k, kb.reshape(Ns * Ts, MOBA_HEADS, MOBA_DH).astype(pool_k.dtype)], axis=0)
        v_src = jnp.concatenate([pool_v, vb.reshape(Ns * Ts, MOBA_HEADS, MOBA_DH).astype(pool_v.dtype)], axis=0)
        moba_o = moba_attention(qb, pos_s, k_src, v_src, rows_s, rel_bias)
        xs = xs + merge_branches(xs, ret_o, ga, moba_o, gate_a, gate_b, ret_norm_w[l], w_pa[l], w_pb[l], w_o[l])
        xs = xs + channel_mix(xs, norm2_w[l], w_ff1[l], w_ff2[l])
        sk.append(kb)
        sv.append(vb)
        ss.append(s_new.astype(state_ret.dtype))
    y_sample = rms_norm(xs, final_norm_w)

    return (y_prompt, y_sample, jnp.stack(pk), jnp.stack(pv), jnp.stack(ps), jnp.stack(sk), jnp.stack(sv), jnp.stack(ss))
```

```python
import functools
import math

import numpy as np
import jax
import jax.numpy as jnp
from jax import lax
from jax.experimental import pallas as pl
from jax.experimental.pallas import tpu as pltpu

F32 = jnp.float32
BF16 = jnp.bfloat16

D_MODEL = 2048
RET_HEADS = 8
RET_DK = 128
RET_DV = 256
MOBA_HEADS = 8
MOBA_DH = 128
MOBA_BLOCK = 256
MOBA_TOPK = 3
REL_BUCKETS = 32
REL_MAX_DIST = 128
ROPE_BASE = 10000.0
NORM_EPS = 1e-6

RET_QK_W = RET_HEADS * RET_DK
RET_V_W = RET_HEADS * RET_DV
MOBA_W = MOBA_HEADS * MOBA_DH
COL_QA = 0
COL_KA = COL_QA + RET_QK_W
COL_VA = COL_KA + RET_QK_W
COL_GA = COL_VA + RET_V_W
COL_QB = COL_GA + RET_V_W
COL_KB = COL_QB + MOBA_W
COL_VB = COL_KB + MOBA_W
COL_GATE_A = COL_VB + MOBA_W
COL_GATE_B = COL_GATE_A + D_MODEL
D_IN = COL_GATE_B + D_MODEL

LANES = 128
SAMPLE_ROWS = 16
MASKED = -1e9
VMEM_LIMIT = 48 * 1024 * 1024

LOG_GAMMA = tuple(math.log(1.0 - 2.0 ** (-5.0 - h)) for h in range(RET_HEADS))

NT_DIMS = (((1,), (1,)), ((), ()))
TN_DIMS = (((0,), (0,)), ((), ()))


def _params(*sem):
    return pltpu.CompilerParams(dimension_semantics=sem, vmem_limit_bytes=VMEM_LIMIT)


def _tile(n, pref):
    t = min(n, pref)
    while n % t:
        t -= 1
    return t


def _rms(x, w):
    return x * lax.rsqrt(jnp.mean(x * x, axis=-1, keepdims=True) + NORM_EPS) * w


def _in_proj_kernel(x_ref, nw_ref, w_ref, o_ref, h_ref):
    @pl.when(pl.program_id(1) == 0)
    def _():
        h_ref[...] = _rms(x_ref[...], nw_ref[...]).astype(BF16)

    o_ref[...] = jnp.dot(h_ref[...], w_ref[...], preferred_element_type=F32)


def in_proj(x, norm_w, w):
    M, K = x.shape
    N = w.shape[1]
    tm, tn = _tile(M, 1024), _tile(N, 1024)
    return pl.pallas_call(
        _in_proj_kernel,
        grid=(M // tm, N // tn),
        in_specs=[pl.BlockSpec((tm, K), lambda i, j: (i, 0)),
                  pl.BlockSpec((1, K), lambda i, j: (0, 0)),
                  pl.BlockSpec((K, tn), lambda i, j: (0, j))],
        out_specs=pl.BlockSpec((tm, tn), lambda i, j: (i, j)),
        out_shape=jax.ShapeDtypeStruct((M, N), F32),
        scratch_shapes=[pltpu.VMEM((tm, K), BF16)],
        compiler_params=_params("arbitrary", "arbitrary"),
        name="in_proj",
    )(x, norm_w.reshape(1, K), w)


def _merge_kernel(ya_ref, mo_ref, ga_ref, gb_ref, wpa_ref, wpb_ref, o_ref):
    ua = jnp.dot(ya_ref[...].astype(BF16), wpa_ref[...], preferred_element_type=F32)
    ub = jnp.dot(mo_ref[...].astype(BF16), wpb_ref[...], preferred_element_type=F32)
    o_ref[...] = (jax.nn.sigmoid(ga_ref[...]) * ua + jax.nn.sigmoid(gb_ref[...]) * ub).astype(BF16)


def merge_gates(y_a, moba_o, z, w_pa, w_pb):
    M = y_a.shape[0]
    tm, tn = _tile(M, 1024), 512
    ga0, gb0 = COL_GATE_A // tn, COL_GATE_B // tn
    return pl.pallas_call(
        _merge_kernel,
        grid=(M // tm, D_MODEL // tn),
        in_specs=[pl.BlockSpec((tm, RET_V_W), lambda i, j: (i, 0)),
                  pl.BlockSpec((tm, MOBA_W), lambda i, j: (i, 0)),
                  pl.BlockSpec((tm, tn), lambda i, j: (i, ga0 + j)),
                  pl.BlockSpec((tm, tn), lambda i, j: (i, gb0 + j)),
                  pl.BlockSpec((RET_V_W, tn), lambda i, j: (0, j)),
                  pl.BlockSpec((MOBA_W, tn), lambda i, j: (0, j))],
        out_specs=pl.BlockSpec((tm, tn), lambda i, j: (i, j)),
        out_shape=jax.ShapeDtypeStruct((M, D_MODEL), BF16),
        compiler_params=_params("arbitrary", "arbitrary"),
        name="merge_gates",
    )(y_a, moba_o, z, z, w_pa, w_pb)


def _out_proj_kernel(m_ref, w_ref, x_ref, o_ref):
    o_ref[...] = x_ref[...] + jnp.dot(m_ref[...], w_ref[...], preferred_element_type=F32)


def out_proj(m, w_o, x):
    M, K = m.shape
    tm, tn = _tile(M, 1024), 512
    return pl.pallas_call(
        _out_proj_kernel,
        grid=(M // tm, D_MODEL // tn),
        in_specs=[pl.BlockSpec((tm, K), lambda i, j: (i, 0)),
                  pl.BlockSpec((K, tn), lambda i, j: (0, j)),
                  pl.BlockSpec((tm, tn), lambda i, j: (i, j))],
        out_specs=pl.BlockSpec((tm, tn), lambda i, j: (i, j)),
        out_shape=jax.ShapeDtypeStruct((M, D_MODEL), F32),
        compiler_params=_params("arbitrary", "arbitrary"),
        name="out_proj",
    )(m, w_o, x)


def _ffn_kernel(x_ref, nw_ref, w1_ref, w2_ref, o_ref, h_ref):
    @pl.when(pl.program_id(1) == 0)
    def _():
        x = x_ref[...]
        h_ref[...] = _rms(x, nw_ref[...]).astype(BF16)
        o_ref[...] = x

    a = jnp.dot(h_ref[...], w1_ref[...], preferred_element_type=F32)
    a = jnp.square(jnp.maximum(a, 0.0)).astype(BF16)
    o_ref[...] += jnp.dot(a, w2_ref[...], preferred_element_type=F32)


def ffn(x, norm_w, w1, w2):
    M, D = x.shape
    FF = w1.shape[1]
    tm, tf = _tile(M, 512), _tile(FF, 1024)
    return pl.pallas_call(
        _ffn_kernel,
        grid=(M // tm, FF // tf),
        in_specs=[pl.BlockSpec((tm, D), lambda i, f: (i, 0)),
                  pl.BlockSpec((1, D), lambda i, f: (0, 0)),
                  pl.BlockSpec((D, tf), lambda i, f: (0, f)),
                  pl.BlockSpec((tf, D), lambda i, f: (f, 0))],
        out_specs=pl.BlockSpec((tm, D), lambda i, f: (i, 0)),
        out_shape=jax.ShapeDtypeStruct((M, D), F32),
        scratch_shapes=[pltpu.VMEM((tm, D), BF16)],
        compiler_params=_params("arbitrary", "arbitrary"),
        name="ffn",
    )(x, norm_w.reshape(1, D), w1, w2)


def _final_norm_kernel(x_ref, nw_ref, o_ref):
    o_ref[...] = _rms(x_ref[...], nw_ref[...])


def final_norm(x, norm_w):
    M, D = x.shape
    tm = _tile(M, 1024)
    return pl.pallas_call(
        _final_norm_kernel,
        grid=(M // tm,),
        in_specs=[pl.BlockSpec((tm, D), lambda i: (i, 0)),
                  pl.BlockSpec((1, D), lambda i: (0, 0))],
        out_specs=pl.BlockSpec((tm, D), lambda i: (i, 0)),
        out_shape=jax.ShapeDtypeStruct((M, D), F32),
        compiler_params=_params("arbitrary"),
        name="final_norm",
    )(x, norm_w.reshape(1, D))


def _rope_tables(pos):
    half = RET_DK // 2
    inv = ROPE_BASE ** (-np.arange(half, dtype=np.float64) / half)
    ang = np.asarray(pos, np.float64)[:, None] * inv[None, :]
    cos = np.concatenate([np.cos(ang), np.cos(ang)], axis=1)
    sin = np.concatenate([-np.sin(ang), np.sin(ang)], axis=1)
    return jnp.asarray(cos, F32), jnp.asarray(sin, F32)


def _retention_kernel(q_ref, k_ref, v_ref, g_ref, cos_ref, sin_ref, nw_ref, s0_ref, y_ref, s_ref, dm_ref,
                      *, c_rows, c_true):
    c = pl.program_id(1)

    @pl.when(c == 0)
    def _():
        s_ref[...] = s0_ref[...]

    @pl.when((pl.program_id(0) == 0) & (c == 0))
    def _():
        i = lax.broadcasted_iota(jnp.int32, (c_rows, c_rows), 0)
        j = lax.broadcasted_iota(jnp.int32, (c_rows, c_rows), 1)
        diff = (i - j).astype(F32)
        for h in range(RET_HEADS):
            dm_ref[h] = jnp.where(diff >= 0, jnp.exp(LOG_GAMMA[h] * jnp.maximum(diff, 0.0)), 0.0)

    cos = cos_ref[...]
    sin = sin_ref[...]
    row = lax.broadcasted_iota(jnp.int32, (c_rows, 1), 0).astype(F32)
    for h in range(RET_HEADS):
        lg = LOG_GAMMA[h]
        q = q_ref[0, :, h * RET_DK:(h + 1) * RET_DK]
        k = k_ref[0, :, h * RET_DK:(h + 1) * RET_DK]
        v = v_ref[0, :, h * RET_DV:(h + 1) * RET_DV].astype(BF16)
        q = q * cos + pltpu.roll(q, RET_DK // 2, 1) * sin
        k = (k * cos + pltpu.roll(k, RET_DK // 2, 1) * sin) * (RET_DK ** -0.5)
        qb = q.astype(BF16)
        scores = lax.dot_general(qb, k.astype(BF16), NT_DIMS, preferred_element_type=F32) * dm_ref[h]
        o = jnp.dot(scores.astype(BF16), v, preferred_element_type=F32)
        s = s_ref[0, h]
        q_dec = jnp.exp(lg * (row + 1.0))
        o = o + jnp.dot(qb, s.astype(BF16), preferred_element_type=F32) * q_dec
        k_dec = jnp.exp(lg * (c_true - 1.0 - row))
        kd = k * k_dec
        if c_true < c_rows:
            kd = jnp.where(row < c_true, kd, 0.0)
        s_ref[0, h] = math.exp(lg * c_true) * s + lax.dot_general(
            kd.astype(BF16), v, TN_DIMS, preferred_element_type=F32)
        y = _rms(o, nw_ref[:, h * RET_DV:(h + 1) * RET_DV])
        g = g_ref[0, :, h * RET_DV:(h + 1) * RET_DV]
        y_ref[0, :, h * RET_DV:(h + 1) * RET_DV] = (y * (g * jax.nn.sigmoid(g))).astype(BF16)


def retention(z3, cos, sin, ret_norm_w, s0, c_rows, c_true):
    N, T, _ = z3.shape
    nc = T // c_rows
    kern = functools.partial(_retention_kernel, c_rows=c_rows, c_true=c_true)
    return pl.pallas_call(
        kern,
        grid=(N, nc),
        in_specs=[pl.BlockSpec((1, c_rows, RET_QK_W), lambda n, c: (n, c, COL_QA // RET_QK_W)),
                  pl.BlockSpec((1, c_rows, RET_QK_W), lambda n, c: (n, c, COL_KA // RET_QK_W)),
                  pl.BlockSpec((1, c_rows, RET_V_W), lambda n, c: (n, c, COL_VA // RET_V_W)),
                  pl.BlockSpec((1, c_rows, RET_V_W), lambda n, c: (n, c, COL_GA // RET_V_W)),
                  pl.BlockSpec((c_rows, RET_DK), lambda n, c: (c, 0)),
                  pl.BlockSpec((c_rows, RET_DK), lambda n, c: (c, 0)),
                  pl.BlockSpec((1, RET_V_W), lambda n, c: (0, 0)),
                  pl.BlockSpec((1, RET_HEADS, RET_DK, RET_DV), lambda n, c: (n, 0, 0, 0))],
        out_specs=[pl.BlockSpec((1, c_rows, RET_V_W), lambda n, c: (n, c, 0)),
                   pl.BlockSpec((1, RET_HEADS, RET_DK, RET_DV), lambda n, c: (n, 0, 0, 0))],
        out_shape=[jax.ShapeDtypeStruct((N, T, RET_V_W), BF16),
                   jax.ShapeDtypeStruct((N, RET_HEADS, RET_DK, RET_DV), F32)],
        scratch_shapes=[pltpu.VMEM((RET_HEADS, c_rows, c_rows), F32)],
        compiler_params=_params("arbitrary", "arbitrary"),
        name="retention",
    )(z3, z3, z3, z3, cos, sin, ret_norm_w.reshape(1, RET_V_W), s0)


def _t5_bucket_np(dist):
    n = np.maximum(dist, 0)
    max_exact = REL_BUCKETS // 2
    nf = np.maximum(n, 1).astype(np.float64)
    large = max_exact + (np.log(nf / max_exact) / math.log(REL_MAX_DIST / max_exact)
                         * (REL_BUCKETS - max_exact)).astype(np.int64)
    large = np.minimum(large, REL_BUCKETS - 1)
    return np.where(n < max_exact, n, large).astype(np.int32)


def _t5_bucket(dist):
    n = jnp.maximum(dist, 0)
    max_exact = REL_BUCKETS // 2
    nf = jnp.maximum(n, 1).astype(F32)
    large = max_exact + (jnp.log(nf / max_exact) / math.log(REL_MAX_DIST / max_exact)
                         * (REL_BUCKETS - max_exact)).astype(jnp.int32)
    large = jnp.minimum(large, REL_BUCKETS - 1)
    return jnp.where(n < max_exact, n, large)


def _bias_lookup(bucket, rb_ref, h, shift):
    out = jnp.zeros(bucket.shape, F32)
    for b in range(REL_BUCKETS):
        out = jnp.where(bucket == b, rb_ref[b, h] - shift, out)
    return out


def _top_blocks(gate, blk, eligible):
    g = jnp.where(eligible, gate, -jnp.inf)
    picks = []
    for _ in range(MOBA_TOPK):
        m = jnp.max(g, axis=1, keepdims=True)
        cand = (g == m) & (m > -jnp.inf)
        idx = jnp.min(jnp.where(cand, blk, LANES), axis=1, keepdims=True)
        idx = jnp.where(idx == LANES, -1, idx)
        picks.append(idx)
        g = jnp.where(blk == idx, -jnp.inf, g)
    return picks


def _moba_bias_kernel(rb_ref, bk_ref, o_ref):
    h = pl.program_id(0)
    far = rb_ref[REL_BUCKETS - 1, h]
    i = lax.broadcasted_iota(jnp.int32, (MOBA_BLOCK, MOBA_BLOCK), 0)
    j = lax.broadcasted_iota(jnp.int32, (MOBA_BLOCK, MOBA_BLOCK), 1)
    o_ref[0, 0] = jnp.where(i >= j, _bias_lookup(bk_ref[0], rb_ref, h, far), MASKED)
    o_ref[0, 1] = _bias_lookup(bk_ref[1], rb_ref, h, far)


def moba_bias_tables(rel_bias):
    d = np.arange(MOBA_BLOCK)[:, None] - np.arange(MOBA_BLOCK)[None, :]
    buckets = jnp.asarray(np.stack([_t5_bucket_np(d), _t5_bucket_np(d + MOBA_BLOCK)]))
    return pl.pallas_call(
        _moba_bias_kernel,
        grid=(MOBA_HEADS,),
        in_specs=[pl.BlockSpec(memory_space=pltpu.SMEM),
                  pl.BlockSpec((2, MOBA_BLOCK, MOBA_BLOCK), lambda h: (0, 0, 0))],
        out_specs=pl.BlockSpec((1, 2, MOBA_BLOCK, MOBA_BLOCK), lambda h: (h, 0, 0, 0)),
        out_shape=jax.ShapeDtypeStruct((MOBA_HEADS, 2, MOBA_BLOCK, MOBA_BLOCK), F32),
        compiler_params=_params("arbitrary"),
        name="moba_bias_tables",
    )(rel_bias, buckets)


def _moba_select_kernel(q_ref, k_ref, v_ref, qa_ref, ka_ref, vb_ref, *, nb):
    q = q_ref[0]
    k = k_ref[0]
    S = q.shape[0]
    kmean = jnp.mean(k.reshape(nb, MOBA_BLOCK, MOBA_DH), axis=1)
    if nb < LANES:
        kmean = jnp.concatenate([kmean, jnp.zeros((LANES - nb, MOBA_DH), F32)], axis=0)
    gate = lax.dot_general(q, kmean, NT_DIMS, precision=lax.Precision.HIGHEST, preferred_element_type=F32)
    blk = lax.broadcasted_iota(jnp.int32, (S, LANES), 1)
    own = lax.shift_right_logical(lax.broadcasted_iota(jnp.int32, (S, LANES), 0), int(math.log2(MOBA_BLOCK)))
    picks = _top_blocks(gate, blk, blk < own)
    allowed = blk == own
    for idx in picks:
        allowed = allowed | (blk == idx)
    qa_ref[0, 0, :, :MOBA_DH] = (q * (MOBA_DH ** -0.5)).astype(BF16)
    qa_ref[0, 0, :, MOBA_DH:] = jnp.where(allowed, 0.0, MASKED).astype(BF16)
    ka_ref[0, 0, :, :MOBA_DH] = k.astype(BF16)
    ka_ref[0, 0, :, MOBA_DH:] = jnp.where(blk == own, 1.0, 0.0).astype(BF16)
    vb_ref[0, 0] = v_ref[0].astype(BF16)


def moba_select(z3):
    B, S, _ = z3.shape
    nb = S // MOBA_BLOCK
    kern = functools.partial(_moba_select_kernel, nb=nb)
    return pl.pallas_call(
        kern,
        grid=(B, MOBA_HEADS),
        in_specs=[pl.BlockSpec((1, S, MOBA_DH), lambda b, h: (b, 0, COL_QB // MOBA_DH + h)),
                  pl.BlockSpec((1, S, MOBA_DH), lambda b, h: (b, 0, COL_KB // MOBA_DH + h)),
                  pl.BlockSpec((1, S, MOBA_DH), lambda b, h: (b, 0, COL_VB // MOBA_DH + h))],
        out_specs=[pl.BlockSpec((1, 1, S, 2 * MOBA_DH), lambda b, h: (b, h, 0, 0)),
                   pl.BlockSpec((1, 1, S, 2 * MOBA_DH), lambda b, h: (b, h, 0, 0)),
                   pl.BlockSpec((1, 1, S, MOBA_DH), lambda b, h: (b, h, 0, 0))],
        out_shape=[jax.ShapeDtypeStruct((B, MOBA_HEADS, S, 2 * MOBA_DH), BF16),
                   jax.ShapeDtypeStruct((B, MOBA_HEADS, S, 2 * MOBA_DH), BF16),
                   jax.ShapeDtypeStruct((B, MOBA_HEADS, S, MOBA_DH), BF16)],
        compiler_params=_params("arbitrary", "arbitrary"),
        name="moba_select",
    )(z3, z3, z3)


def _moba_flash_kernel(q_ref, k_ref, v_ref, bias_ref, o_ref):
    i = pl.program_id(2)
    q = q_ref[0, 0]

    def logits_and_values(j):
        off = pl.multiple_of(j * MOBA_BLOCK, MOBA_BLOCK)
        kt = k_ref[0, 0, pl.ds(off, MOBA_BLOCK), :]
        vt = v_ref[0, 0, pl.ds(off, MOBA_BLOCK), :]
        return lax.dot_general(q, kt, NT_DIMS, preferred_element_type=F32), vt

    def update(carry, s, vt):
        m, l, acc = carry
        m_new = jnp.maximum(m, jnp.max(s, axis=1, keepdims=True))
        a = jnp.exp(m - m_new)
        p = jnp.exp(s - m_new)
        l = a * l + jnp.sum(p, axis=1, keepdims=True)
        acc = a * acc + jnp.dot(p.astype(BF16), vt, preferred_element_type=F32)
        return m_new, l, acc

    s, vt = logits_and_values(i)
    s = s + bias_ref[0, 0]
    m = jnp.max(s, axis=1, keepdims=True)
    p = jnp.exp(s - m)
    carry = (m, jnp.sum(p, axis=1, keepdims=True), jnp.dot(p.astype(BF16), vt, preferred_element_type=F32))

    s, vt = logits_and_values(jnp.maximum(i - 1, 0))
    s = s + bias_ref[0, 1] + jnp.where(i >= 1, 0.0, MASKED)
    carry = update(carry, s, vt)

    def far_block(j, carry):
        s, vt = logits_and_values(j)
        return update(carry, s, vt)

    m, l, acc = lax.fori_loop(0, jnp.maximum(i - 1, 0), far_block, carry)
    o_ref[0] = (acc / l).astype(BF16)


def moba_flash(q_aug, k_aug, v, bias):
    B, H, S, _ = q_aug.shape
    nb = S // MOBA_BLOCK
    return pl.pallas_call(
        _moba_flash_kernel,
        grid=(B, H, nb),
        in_specs=[pl.BlockSpec((1, 1, MOBA_BLOCK, 2 * MOBA_DH), lambda b, h, i: (b, h, i, 0)),
                  pl.BlockSpec((1, 1, S, 2 * MOBA_DH), lambda b, h, i: (b, h, 0, 0)),
                  pl.BlockSpec((1, 1, S, MOBA_DH), lambda b, h, i: (b, h, 0, 0)),
                  pl.BlockSpec((1, 2, MOBA_BLOCK, MOBA_BLOCK), lambda b, h, i: (h, 0, 0, 0))],
        out_specs=pl.BlockSpec((1, MOBA_BLOCK, MOBA_DH), lambda b, h, i: (b, i, h)),
        out_shape=jax.ShapeDtypeStruct((B, S, MOBA_W), BF16),
        compiler_params=_params("arbitrary", "arbitrary", "arbitrary"),
        name="moba_flash",
    )(q_aug, k_aug, v, bias)


def _page_sum_kernel(c_ref, o_ref):
    o_ref[...] = jnp.sum(c_ref[...], axis=1)


def page_sums(cache):
    P, page, H, dh = cache.shape
    g = _tile(P, 8)
    return pl.pallas_call(
        _page_sum_kernel,
        grid=(P // g,),
        in_specs=[pl.BlockSpec((g, page, H, dh), lambda i: (i, 0, 0, 0))],
        out_specs=pl.BlockSpec((g, H, dh), lambda i: (i, 0, 0)),
        out_shape=jax.ShapeDtypeStruct((P, H, dh), F32),
        compiler_params=_params("arbitrary"),
        name="page_sums",
    )(cache)


def _sample_select_kernel(pt_ref, ps_ref, q_ref, sel_ref, km_ref, *, nb_past, pages_per_block, past_len):
    n = pl.program_id(0)
    km_ref[...] = jnp.zeros(km_ref.shape, F32)

    def block_mean(b, _):
        acc = jnp.zeros((MOBA_HEADS, MOBA_DH), F32)
        for p in range(pages_per_block):
            acc = acc + ps_ref[0, pt_ref[n, b * pages_per_block + p]]
        acc = acc * (1.0 / MOBA_BLOCK)
        for h in range(MOBA_HEADS):
            km_ref[h, pl.ds(b, 1), :] = acc[h:h + 1, :]
        return 0

    lax.fori_loop(0, nb_past, block_mean, 0)

    rows = q_ref.shape[1]
    blk = lax.broadcasted_iota(jnp.int32, (rows, LANES), 1)
    pos = past_len + lax.broadcasted_iota(jnp.int32, (rows, LANES), 0)
    own = lax.shift_right_logical(pos, int(math.log2(MOBA_BLOCK)))
    for h in range(MOBA_HEADS):
        q = q_ref[0, :, h * MOBA_DH:(h + 1) * MOBA_DH]
        km = km_ref[h]
        gate = lax.dot_general(q, km, NT_DIMS, precision=lax.Precision.HIGHEST, preferred_element_type=F32)
        picks = _top_blocks(gate, blk, (blk < own) & (blk < nb_past))
        out = jnp.full((rows, LANES), -1, jnp.int32)
        for r, idx in enumerate(picks):
            out = jnp.where(blk == r, idx, out)
        sel_ref[0, h] = out


def sample_select(page_table, psums, layer, z3, past_len, page_size):
    Ns, rows, _ = z3.shape
    n_phys = psums.shape[1]
    pages_per_block = MOBA_BLOCK // page_size
    nb_past = past_len // MOBA_BLOCK
    assert nb_past <= LANES
    kern = functools.partial(_sample_select_kernel, nb_past=nb_past, pages_per_block=pages_per_block,
                             past_len=past_len)
    return pl.pallas_call(
        kern,
        grid=(Ns,),
        in_specs=[pl.BlockSpec(memory_space=pltpu.SMEM),
                  pl.BlockSpec((1, n_phys, MOBA_HEADS, MOBA_DH), lambda n: (layer, 0, 0, 0)),
                  pl.BlockSpec((1, rows, MOBA_W), lambda n: (n, 0, COL_QB // MOBA_W))],
        out_specs=pl.BlockSpec((1, MOBA_HEADS, rows, LANES), lambda n: (n, 0, 0, 0)),
        out_shape=jax.ShapeDtypeStruct((Ns, MOBA_HEADS, rows, LANES), jnp.int32),
        scratch_shapes=[pltpu.VMEM((MOBA_HEADS, LANES, MOBA_DH), F32)],
        compiler_params=_params("arbitrary"),
        name="sample_select",
    )(page_table, psums, z3)


def _sample_attn_kernel(pg_ref, bk_ref, q_ref, kc_ref, vc_ref, kn_ref, vn_ref, rb_ref, o_ref,
                        m_ref, l_ref, acc_ref, *, t_true, past_len, page_size, pages_per_block):
    n, h, j = pl.program_id(0), pl.program_id(1), pl.program_id(2)
    rows = q_ref.shape[1]
    q = (q_ref[0] * (MOBA_DH ** -0.5)).astype(BF16)

    def attend(k, v, qpos, kpos, valid, exact_bias):
        s = lax.dot_general(q, k.astype(BF16), NT_DIMS, preferred_element_type=F32)
        far = rb_ref[REL_BUCKETS - 1, h]
        bias = lax.cond(exact_bias,
                        lambda: _bias_lookup(_t5_bucket(qpos - kpos), rb_ref, h, 0.0),
                        lambda: jnp.full(s.shape, far, F32))
        return jnp.where(valid, s + bias, MASKED), v.astype(BF16)

    @pl.when(j == 0)
    def _():
        row = lax.broadcasted_iota(jnp.int32, (rows, LANES), 0)
        lane = lax.broadcasted_iota(jnp.int32, (rows, LANES), 1)
        pad = jnp.zeros((LANES - rows, MOBA_DH), F32)
        s, v = attend(jnp.concatenate([kn_ref[0], pad], axis=0), jnp.concatenate([vn_ref[0], pad], axis=0),
                      past_len + row, past_len + lane, (lane <= row) & (lane < t_true), True)
        m = jnp.max(s, axis=1, keepdims=True)
        p = jnp.exp(s - m)
        m_ref[...] = m
        l_ref[...] = jnp.sum(p, axis=1, keepdims=True)
        acc_ref[...] = jnp.dot(p.astype(BF16), v, preferred_element_type=F32)

    @pl.when(j >= 1)
    def _():
        cols = page_size * MOBA_HEADS
        row = lax.broadcasted_iota(jnp.int32, (rows, cols), 0)
        col = lax.broadcasted_iota(jnp.int32, (rows, cols), 1)
        slot = (j - 1) // pages_per_block
        part = (j - 1) % pages_per_block
        base = (n * MOBA_HEADS + h) * (t_true * MOBA_TOPK)
        b = bk_ref[base + slot]
        kpos0 = b * MOBA_BLOCK + part * page_size
        kpos = kpos0 + lax.shift_right_logical(col, int(math.log2(MOBA_HEADS)))
        valid = (row == slot // MOBA_TOPK) & ((col & (MOBA_HEADS - 1)) == h) & (b >= 0)
        near = past_len - (kpos0 + page_size - 1) < REL_MAX_DIST
        s, v = attend(kc_ref[0], vc_ref[0], past_len + row, kpos, valid, near)
        m = m_ref[...]
        m_new = jnp.maximum(m, jnp.max(s, axis=1, keepdims=True))
        a = jnp.exp(m - m_new)
        p = jnp.exp(s - m_new)
        m_ref[...] = m_new
        l_ref[...] = a * l_ref[...] + jnp.sum(p, axis=1, keepdims=True)
        acc_ref[...] = a * acc_ref[...] + jnp.dot(p.astype(BF16), v, preferred_element_type=F32)

    @pl.when(j == pl.num_programs(2) - 1)
    def _():
        o_ref[0] = acc_ref[...] / l_ref[...]


def sample_attention(pages, blocks, z3, cache_k, cache_v, layer, rel_bias, t_true, past_len):
    Ns, rows, _ = z3.shape
    page_size = cache_k.shape[2] // MOBA_HEADS
    assert rows <= LANES
    ppb = MOBA_BLOCK // page_size
    steps = t_true * MOBA_TOPK * ppb
    kern = functools.partial(_sample_attn_kernel, t_true=t_true, past_len=past_len, page_size=page_size,
                             pages_per_block=ppb)

    def page_map(n, h, j, pg, bk):
        return (layer, pg[(n * MOBA_HEADS + h) * steps + jnp.maximum(j - 1, 0)], 0, 0)

    grid_spec = pltpu.PrefetchScalarGridSpec(
        num_scalar_prefetch=2,
        grid=(Ns, MOBA_HEADS, steps + 1),
        in_specs=[pl.BlockSpec((1, rows, MOBA_DH), lambda n, h, j, pg, bk: (n, 0, COL_QB // MOBA_DH + h)),
                  pl.BlockSpec((None, 1, page_size * MOBA_HEADS, MOBA_DH), page_map),
                  pl.BlockSpec((None, 1, page_size * MOBA_HEADS, MOBA_DH), page_map),
                  pl.BlockSpec((1, rows, MOBA_DH), lambda n, h, j, pg, bk: (n, 0, COL_KB // MOBA_DH + h)),
                  pl.BlockSpec((1, rows, MOBA_DH), lambda n, h, j, pg, bk: (n, 0, COL_VB // MOBA_DH + h)),
                  pl.BlockSpec(memory_space=pltpu.SMEM)],
        out_specs=pl.BlockSpec((1, rows, MOBA_DH), lambda n, h, j, pg, bk: (n, 0, h)),
        scratch_shapes=[pltpu.VMEM((rows, 1), F32), pltpu.VMEM((rows, 1), F32), pltpu.VMEM((rows, MOBA_DH), F32)],
    )
    return pl.pallas_call(
        kern,
        grid_spec=grid_spec,
        out_shape=jax.ShapeDtypeStruct((Ns, rows, MOBA_W), F32),
        compiler_params=_params("arbitrary", "arbitrary", "arbitrary"),
        name="sample_attention",
    )(pages, blocks, z3, cache_k, cache_v, z3, z3, rel_bias)


def _layer_tail(x, z, y_a, moba_o, wl):
    m = merge_gates(y_a, moba_o, z, wl["w_pa"], wl["w_pb"])
    x = out_proj(m, wl["w_o"], x)
    return ffn(x, wl["norm2_w"], wl["w_ff1"], wl["w_ff2"])


def kernel(x_prompt, x_sample, cache_k, cache_v, state_ret, page_table, rel_bias, norm1_w, w_in, ret_norm_w,
           w_pa, w_pb, w_o, norm2_w, w_ff1, w_ff2, final_norm_w):
    depth = w_in.shape[0]
    B, S, D = x_prompt.shape
    Ns, Ts, _ = x_sample.shape
    n_phys, page_size = cache_k.shape[1], cache_k.shape[2]
    n_pages = page_table.shape[1]
    past_len = n_pages * page_size
    assert D == D_MODEL and S % MOBA_BLOCK == 0 and past_len % MOBA_BLOCK == 0 and Ts <= SAMPLE_ROWS
    ppb = MOBA_BLOCK // page_size

    weights = [dict(norm1_w=norm1_w[l], w_in=w_in[l].astype(BF16), ret_norm_w=ret_norm_w[l],
                    w_pa=w_pa[l].astype(BF16), w_pb=w_pb[l].astype(BF16), w_o=w_o[l].astype(BF16),
                    norm2_w=norm2_w[l], w_ff1=w_ff1[l].astype(BF16), w_ff2=w_ff2[l].astype(BF16))
               for l in range(depth)]

    ret_chunk = _tile(S, 256)
    cos_p, sin_p = _rope_tables(np.arange(S))
    bias_tables = moba_bias_tables(rel_bias)
    zero_state = jnp.zeros((B, RET_HEADS, RET_DK, RET_DV), F32)
    xp = x_prompt.reshape(B * S, D)
    pk, pv, ps = [], [], []
    for wl in weights:
        z = in_proj(xp, wl["norm1_w"], wl["w_in"])
        z3 = z.reshape(B, S, D_IN)
        y_a, s_fin = retention(z3, cos_p, sin_p, wl["ret_norm_w"], zero_state, ret_chunk, ret_chunk)
        q_aug, k_aug, v_b = moba_select(z3)
        moba_o = moba_flash(q_aug, k_aug, v_b, bias_tables)
        xp = _layer_tail(xp, z, y_a.reshape(B * S, RET_V_W), moba_o.reshape(B * S, MOBA_W), wl)
        pk.append(z3[:, :, COL_KB:COL_KB + MOBA_W].reshape(B, S, MOBA_HEADS, MOBA_DH))
        pv.append(z3[:, :, COL_VB:COL_VB + MOBA_W].reshape(B, S, MOBA_HEADS, MOBA_DH))
        ps.append(s_fin)
    y_prompt = final_norm(xp, final_norm_w).reshape(B, S, D)

    R = SAMPLE_ROWS
    cos_s, sin_s = _rope_tables(past_len + np.arange(R))
    ck = cache_k.reshape(depth, n_phys, page_size * MOBA_HEADS, MOBA_DH)
    cv = cache_v.reshape(depth, n_phys, page_size * MOBA_HEADS, MOBA_DH)
    psums = page_sums(cache_k.reshape(depth * n_phys, page_size, MOBA_HEADS, MOBA_DH))
    psums = psums.reshape(depth, n_phys, MOBA_HEADS, MOBA_DH)
    xs = jnp.zeros((Ns, R, D), F32).at[:, :Ts].set(x_sample).reshape(Ns * R, D)
    sk, sv, ss = [], [], []
    for l, wl in enumerate(weights):
        z = in_proj(xs, wl["norm1_w"], wl["w_in"])
        z3 = z.reshape(Ns, R, D_IN)
        y_a, s_new = retention(z3, cos_s, sin_s, wl["ret_norm_w"], state_ret[l], R, Ts)
        sel = sample_select(page_table, psums, l, z3, past_len, page_size)
        blocks = sel[:, :, :Ts, :MOBA_TOPK]
        logical = jnp.maximum(blocks, 0)[..., None] * ppb + jnp.arange(ppb)
        pages = jnp.take_along_axis(page_table[:, None, :], logical.reshape(Ns, MOBA_HEADS, -1), axis=2)
        moba_o = sample_attention(pages.reshape(-1), blocks.reshape(-1), z3, ck, cv, l, rel_bias, Ts, past_len)
        xs = _layer_tail(xs, z, y_a.reshape(Ns * R, RET_V_W), moba_o.reshape(Ns * R, MOBA_W), wl)
        sk.append(z3[:, :Ts, COL_KB:COL_KB + MOBA_W].reshape(Ns, Ts, MOBA_HEADS, MOBA_DH))
        sv.append(z3[:, :Ts, COL_VB:COL_VB + MOBA_W].reshape(Ns, Ts, MOBA_HEADS, MOBA_DH))
        ss.append(s_new)
    y_sample = final_norm(xs, final_norm_w).reshape(Ns, R, D)[:, :Ts]

    return (y_prompt, y_sample, jnp.stack(pk), jnp.stack(pv), jnp.stack(ps),
            jnp.stack(sk), jnp.stack(sv), jnp.stack(ss))
```

```python
import functools
import math

import numpy as np
import jax
import jax.numpy as jnp
from jax import lax
from jax.experimental import pallas as pl
from jax.experimental.pallas import tpu as pltpu

F32 = jnp.float32
BF16 = jnp.bfloat16

D_MODEL = 2048
RET_HEADS = 8
RET_DK = 128
RET_DV = 256
MOBA_HEADS = 8
MOBA_DH = 128
MOBA_BLOCK = 256
MOBA_TOPK = 3
REL_BUCKETS = 32
REL_MAX_DIST = 128
ROPE_BASE = 10000.0
NORM_EPS = 1e-6

RET_QK_W = RET_HEADS * RET_DK
RET_V_W = RET_HEADS * RET_DV
MOBA_W = MOBA_HEADS * MOBA_DH
COL_QA = 0
COL_KA = COL_QA + RET_QK_W
COL_VA = COL_KA + RET_QK_W
COL_GA = COL_VA + RET_V_W
COL_QB = COL_GA + RET_V_W
COL_KB = COL_QB + MOBA_W
COL_VB = COL_KB + MOBA_W
COL_GATE_A = COL_VB + MOBA_W
COL_GATE_B = COL_GATE_A + D_MODEL
D_IN = COL_GATE_B + D_MODEL

LANES = 128
SAMPLE_ROWS = 16
SAMPLE_PAGES_PER_STEP = 4
MASKED = -1e9
LOG2_E = math.log2(math.e)
VMEM_LIMIT = 48 * 1024 * 1024

LOG_GAMMA = tuple(math.log(1.0 - 2.0 ** (-5.0 - h)) for h in range(RET_HEADS))

NT_DIMS = (((1,), (1,)), ((), ()))
TN_DIMS = (((0,), (0,)), ((), ()))


def _params(*sem):
    return pltpu.CompilerParams(dimension_semantics=sem, vmem_limit_bytes=VMEM_LIMIT)


def _tile(n, pref):
    t = min(n, pref)
    while n % t:
        t -= 1
    return t


def _rms(x, w):
    return x * lax.rsqrt(jnp.mean(x * x, axis=-1, keepdims=True) + NORM_EPS) * w


def _in_proj_kernel(x_ref, nw_ref, w_ref, o_ref, h_ref):
    @pl.when(pl.program_id(1) == 0)
    def _():
        h_ref[...] = _rms(x_ref[...], nw_ref[...]).astype(BF16)

    o_ref[...] = jnp.dot(h_ref[...], w_ref[...], preferred_element_type=F32)


def _layer_vec(w):
    return w.reshape(w.shape[0], 1, w.shape[1])


def in_proj(x, norm_w, w, layer):
    M, K = x.shape
    N = w.shape[2]
    tm, tn = _tile(M, 1024), _tile(N, 1024)
    return pl.pallas_call(
        _in_proj_kernel,
        grid=(M // tm, N // tn),
        in_specs=[pl.BlockSpec((tm, K), lambda i, j: (i, 0)),
                  pl.BlockSpec((None, 1, K), lambda i, j: (layer, 0, 0)),
                  pl.BlockSpec((None, K, tn), lambda i, j: (layer, 0, j))],
        out_specs=pl.BlockSpec((tm, tn), lambda i, j: (i, j)),
        out_shape=jax.ShapeDtypeStruct((M, N), F32),
        scratch_shapes=[pltpu.VMEM((tm, K), BF16)],
        compiler_params=_params("arbitrary", "arbitrary"),
        name="in_proj",
    )(x, _layer_vec(norm_w), w)


def _merge_kernel(ya_ref, mo_ref, ga_ref, gb_ref, wpa_ref, wpb_ref, o_ref):
    ua = jnp.dot(ya_ref[...].astype(BF16), wpa_ref[...], preferred_element_type=F32)
    ub = jnp.dot(mo_ref[...].astype(BF16), wpb_ref[...], preferred_element_type=F32)
    o_ref[...] = (jax.nn.sigmoid(ga_ref[...]) * ua + jax.nn.sigmoid(gb_ref[...]) * ub).astype(BF16)


def merge_gates(y_a, moba_o, z, w_pa, w_pb, layer):
    M = y_a.shape[0]
    tm, tn = _tile(M, 1024), 512
    ga0, gb0 = COL_GATE_A // tn, COL_GATE_B // tn
    return pl.pallas_call(
        _merge_kernel,
        grid=(M // tm, D_MODEL // tn),
        in_specs=[pl.BlockSpec((tm, RET_V_W), lambda i, j: (i, 0)),
                  pl.BlockSpec((tm, MOBA_W), lambda i, j: (i, 0)),
                  pl.BlockSpec((tm, tn), lambda i, j: (i, ga0 + j)),
                  pl.BlockSpec((tm, tn), lambda i, j: (i, gb0 + j)),
                  pl.BlockSpec((None, RET_V_W, tn), lambda i, j: (layer, 0, j)),
                  pl.BlockSpec((None, MOBA_W, tn), lambda i, j: (layer, 0, j))],
        out_specs=pl.BlockSpec((tm, tn), lambda i, j: (i, j)),
        out_shape=jax.ShapeDtypeStruct((M, D_MODEL), BF16),
        compiler_params=_params("arbitrary", "arbitrary"),
        name="merge_gates",
    )(y_a, moba_o, z, z, w_pa, w_pb)


def _out_proj_kernel(m_ref, w_ref, x_ref, o_ref):
    o_ref[...] = x_ref[...] + jnp.dot(m_ref[...], w_ref[...], preferred_element_type=F32)


def out_proj(m, w_o, x, layer):
    M, K = m.shape
    tm, tn = _tile(M, 1024), 512
    return pl.pallas_call(
        _out_proj_kernel,
        grid=(M // tm, D_MODEL // tn),
        in_specs=[pl.BlockSpec((tm, K), lambda i, j: (i, 0)),
                  pl.BlockSpec((None, K, tn), lambda i, j: (layer, 0, j)),
                  pl.BlockSpec((tm, tn), lambda i, j: (i, j))],
        out_specs=pl.BlockSpec((tm, tn), lambda i, j: (i, j)),
        out_shape=jax.ShapeDtypeStruct((M, D_MODEL), F32),
        compiler_params=_params("arbitrary", "arbitrary"),
        name="out_proj",
    )(m, w_o, x)


def _ffn_kernel(x_ref, nw_ref, w1_ref, w2_ref, o_ref, h_ref):
    @pl.when(pl.program_id(1) == 0)
    def _():
        x = x_ref[...]
        h_ref[...] = _rms(x, nw_ref[...]).astype(BF16)
        o_ref[...] = x

    a = jnp.dot(h_ref[...], w1_ref[...], preferred_element_type=F32)
    a = jnp.square(jnp.maximum(a, 0.0)).astype(BF16)
    o_ref[...] += jnp.dot(a, w2_ref[...], preferred_element_type=F32)


def ffn(x, norm_w, w1, w2, layer):
    M, D = x.shape
    FF = w1.shape[2]
    tm, tf = _tile(M, 512), _tile(FF, 1024)
    return pl.pallas_call(
        _ffn_kernel,
        grid=(M // tm, FF // tf),
        in_specs=[pl.BlockSpec((tm, D), lambda i, f: (i, 0)),
                  pl.BlockSpec((None, 1, D), lambda i, f: (layer, 0, 0)),
                  pl.BlockSpec((None, D, tf), lambda i, f: (layer, 0, f)),
                  pl.BlockSpec((None, tf, D), lambda i, f: (layer, f, 0))],
        out_specs=pl.BlockSpec((tm, D), lambda i, f: (i, 0)),
        out_shape=jax.ShapeDtypeStruct((M, D), F32),
        scratch_shapes=[pltpu.VMEM((tm, D), BF16)],
        compiler_params=_params("arbitrary", "arbitrary"),
        name="ffn",
    )(x, _layer_vec(norm_w), w1, w2)


def _final_norm_kernel(x_ref, nw_ref, o_ref):
    o_ref[...] = _rms(x_ref[...], nw_ref[...])


def final_norm(x, norm_w):
    M, D = x.shape
    tm = _tile(M, 1024)
    return pl.pallas_call(
        _final_norm_kernel,
        grid=(M // tm,),
        in_specs=[pl.BlockSpec((tm, D), lambda i: (i, 0)),
                  pl.BlockSpec((1, D), lambda i: (0, 0))],
        out_specs=pl.BlockSpec((tm, D), lambda i: (i, 0)),
        out_shape=jax.ShapeDtypeStruct((M, D), F32),
        compiler_params=_params("arbitrary"),
        name="final_norm",
    )(x, norm_w.reshape(1, D))


def _rope_tables(pos):
    half = RET_DK // 2
    inv = ROPE_BASE ** (-np.arange(half, dtype=np.float64) / half)
    ang = np.asarray(pos, np.float64)[:, None] * inv[None, :]
    cos = np.concatenate([np.cos(ang), np.cos(ang)], axis=1)
    sin = np.concatenate([-np.sin(ang), np.sin(ang)], axis=1)
    return jnp.asarray(cos, F32), jnp.asarray(sin, F32)


def _retention_kernel(q_ref, k_ref, v_ref, g_ref, cos_ref, sin_ref, nw_ref, s0_ref, y_ref, s_ref, dm_ref,
                      *, c_rows, c_true):
    c = pl.program_id(1)

    @pl.when(c == 0)
    def _():
        s_ref[...] = s0_ref[...]

    @pl.when((pl.program_id(0) == 0) & (c == 0))
    def _():
        i = lax.broadcasted_iota(jnp.int32, (c_rows, c_rows), 0)
        j = lax.broadcasted_iota(jnp.int32, (c_rows, c_rows), 1)
        diff = (i - j).astype(F32)
        for h in range(RET_HEADS):
            dm_ref[h] = jnp.where(diff >= 0, jnp.exp(LOG_GAMMA[h] * jnp.maximum(diff, 0.0)), 0.0)

    cos = cos_ref[...]
    sin = sin_ref[...]
    row = lax.broadcasted_iota(jnp.int32, (c_rows, 1), 0).astype(F32)
    for h in range(RET_HEADS):
        lg = LOG_GAMMA[h]
        q = q_ref[0, :, h * RET_DK:(h + 1) * RET_DK]
        k = k_ref[0, :, h * RET_DK:(h + 1) * RET_DK]
        v = v_ref[0, :, h * RET_DV:(h + 1) * RET_DV].astype(BF16)
        q = q * cos + pltpu.roll(q, RET_DK // 2, 1) * sin
        k = (k * cos + pltpu.roll(k, RET_DK // 2, 1) * sin) * (RET_DK ** -0.5)
        qb = q.astype(BF16)
        scores = lax.dot_general(qb, k.astype(BF16), NT_DIMS, preferred_element_type=F32) * dm_ref[h]
        o = jnp.dot(scores.astype(BF16), v, preferred_element_type=F32)
        s = s_ref[0, h]
        q_dec = jnp.exp(lg * (row + 1.0))
        o = o + jnp.dot(qb, s.astype(BF16), preferred_element_type=F32) * q_dec
        k_dec = jnp.exp(lg * (c_true - 1.0 - row))
        kd = k * k_dec
        if c_true < c_rows:
            kd = jnp.where(row < c_true, kd, 0.0)
        s_ref[0, h] = math.exp(lg * c_true) * s + lax.dot_general(
            kd.astype(BF16), v, TN_DIMS, preferred_element_type=F32)
        y = _rms(o, nw_ref[:, h * RET_DV:(h + 1) * RET_DV])
        g = g_ref[0, :, h * RET_DV:(h + 1) * RET_DV]
        y_ref[0, :, h * RET_DV:(h + 1) * RET_DV] = (y * (g * jax.nn.sigmoid(g))).astype(BF16)


def retention(z3, cos, sin, ret_norm_w, layer, s0, s0_layer, c_rows, c_true):
    N, T, _ = z3.shape
    nc = T // c_rows
    kern = functools.partial(_retention_kernel, c_rows=c_rows, c_true=c_true)
    return pl.pallas_call(
        kern,
        grid=(N, nc),
        in_specs=[pl.BlockSpec((1, c_rows, RET_QK_W), lambda n, c: (n, c, COL_QA // RET_QK_W)),
                  pl.BlockSpec((1, c_rows, RET_QK_W), lambda n, c: (n, c, COL_KA // RET_QK_W)),
                  pl.BlockSpec((1, c_rows, RET_V_W), lambda n, c: (n, c, COL_VA // RET_V_W)),
                  pl.BlockSpec((1, c_rows, RET_V_W), lambda n, c: (n, c, COL_GA // RET_V_W)),
                  pl.BlockSpec((c_rows, RET_DK), lambda n, c: (c, 0)),
                  pl.BlockSpec((c_rows, RET_DK), lambda n, c: (c, 0)),
                  pl.BlockSpec((None, 1, RET_V_W), lambda n, c: (layer, 0, 0)),
                  pl.BlockSpec((None, 1, RET_HEADS, RET_DK, RET_DV), lambda n, c: (s0_layer, n, 0, 0, 0))],
        out_specs=[pl.BlockSpec((1, c_rows, RET_V_W), lambda n, c: (n, c, 0)),
                   pl.BlockSpec((1, RET_HEADS, RET_DK, RET_DV), lambda n, c: (n, 0, 0, 0))],
        out_shape=[jax.ShapeDtypeStruct((N, T, RET_V_W), BF16),
                   jax.ShapeDtypeStruct((N, RET_HEADS, RET_DK, RET_DV), F32)],
        scratch_shapes=[pltpu.VMEM((RET_HEADS, c_rows, c_rows), F32)],
        compiler_params=_params("arbitrary", "arbitrary"),
        name="retention",
    )(z3, z3, z3, z3, cos, sin, _layer_vec(ret_norm_w), s0)


def _t5_bucket_np(dist):
    n = np.maximum(dist, 0)
    max_exact = REL_BUCKETS // 2
    nf = np.maximum(n, 1).astype(np.float64)
    large = max_exact + (np.log(nf / max_exact) / math.log(REL_MAX_DIST / max_exact)
                         * (REL_BUCKETS - max_exact)).astype(np.int64)
    large = np.minimum(large, REL_BUCKETS - 1)
    return np.where(n < max_exact, n, large).astype(np.int32)


def _t5_bucket(dist):
    n = jnp.maximum(dist, 0)
    max_exact = REL_BUCKETS // 2
    nf = jnp.maximum(n, 1).astype(F32)
    large = max_exact + (jnp.log(nf / max_exact) / math.log(REL_MAX_DIST / max_exact)
                         * (REL_BUCKETS - max_exact)).astype(jnp.int32)
    large = jnp.minimum(large, REL_BUCKETS - 1)
    return jnp.where(n < max_exact, n, large)


def _bias_lookup(bucket, rb_ref, h, shift):
    out = jnp.zeros(bucket.shape, F32)
    for b in range(REL_BUCKETS):
        out = jnp.where(bucket == b, rb_ref[b, h] - shift, out)
    return out


def _top_blocks(gate, blk, eligible):
    g = jnp.where(eligible, gate, -jnp.inf)
    picks = []
    for _ in range(MOBA_TOPK):
        m = jnp.max(g, axis=1, keepdims=True)
        cand = (g == m) & (m > -jnp.inf)
        idx = jnp.min(jnp.where(cand, blk, LANES), axis=1, keepdims=True)
        idx = jnp.where(idx == LANES, -1, idx)
        picks.append(idx)
        g = jnp.where(blk == idx, -jnp.inf, g)
    return picks


def _moba_bias_kernel(rb_ref, bk_ref, o_ref):
    h = pl.program_id(0)
    far = rb_ref[REL_BUCKETS - 1, h]
    i = lax.broadcasted_iota(jnp.int32, (MOBA_BLOCK, MOBA_BLOCK), 0)
    j = lax.broadcasted_iota(jnp.int32, (MOBA_BLOCK, MOBA_BLOCK), 1)
    o_ref[0, 0] = jnp.where(i >= j, _bias_lookup(bk_ref[0], rb_ref, h, far) * LOG2_E, MASKED)
    o_ref[0, 1] = _bias_lookup(bk_ref[1], rb_ref, h, far) * LOG2_E


def moba_bias_tables(rel_bias):
    d = np.arange(MOBA_BLOCK)[:, None] - np.arange(MOBA_BLOCK)[None, :]
    buckets = jnp.asarray(np.stack([_t5_bucket_np(d), _t5_bucket_np(d + MOBA_BLOCK)]))
    return pl.pallas_call(
        _moba_bias_kernel,
        grid=(MOBA_HEADS,),
        in_specs=[pl.BlockSpec(memory_space=pltpu.SMEM),
                  pl.BlockSpec((2, MOBA_BLOCK, MOBA_BLOCK), lambda h: (0, 0, 0))],
        out_specs=pl.BlockSpec((1, 2, MOBA_BLOCK, MOBA_BLOCK), lambda h: (h, 0, 0, 0)),
        out_shape=jax.ShapeDtypeStruct((MOBA_HEADS, 2, MOBA_BLOCK, MOBA_BLOCK), F32),
        compiler_params=_params("arbitrary"),
        name="moba_bias_tables",
    )(rel_bias, buckets)


def _moba_select_kernel(q_ref, k_ref, v_ref, qa_ref, ka_ref, vb_ref, *, nb):
    q = q_ref[0]
    k = k_ref[0]
    S = q.shape[0]
    kmean = jnp.mean(k.reshape(nb, MOBA_BLOCK, MOBA_DH), axis=1)
    if nb < LANES:
        kmean = jnp.concatenate([kmean, jnp.zeros((LANES - nb, MOBA_DH), F32)], axis=0)
    gate = lax.dot_general(q, kmean, NT_DIMS, precision=lax.Precision.HIGHEST, preferred_element_type=F32)
    blk = lax.broadcasted_iota(jnp.int32, (S, LANES), 1)
    own = lax.shift_right_logical(lax.broadcasted_iota(jnp.int32, (S, LANES), 0), int(math.log2(MOBA_BLOCK)))
    picks = _top_blocks(gate, blk, blk < own)
    allowed = blk == own
    for idx in picks:
        allowed = allowed | (blk == idx)
    qa_ref[0, 0, :, :MOBA_DH] = (q * (MOBA_DH ** -0.5 * LOG2_E)).astype(BF16)
    qa_ref[0, 0, :, MOBA_DH:] = jnp.where(allowed, 0.0, MASKED).astype(BF16)
    ka_ref[0, 0, :, :MOBA_DH] = k.astype(BF16)
    ka_ref[0, 0, :, MOBA_DH:] = jnp.where(blk == own, 1.0, 0.0).astype(BF16)
    vb_ref[0, 0] = v_ref[0].astype(BF16)


def moba_select(z3):
    B, S, _ = z3.shape
    nb = S // MOBA_BLOCK
    kern = functools.partial(_moba_select_kernel, nb=nb)
    return pl.pallas_call(
        kern,
        grid=(B, MOBA_HEADS),
        in_specs=[pl.BlockSpec((1, S, MOBA_DH), lambda b, h: (b, 0, COL_QB // MOBA_DH + h)),
                  pl.BlockSpec((1, S, MOBA_DH), lambda b, h: (b, 0, COL_KB // MOBA_DH + h)),
                  pl.BlockSpec((1, S, MOBA_DH), lambda b, h: (b, 0, COL_VB // MOBA_DH + h))],
        out_specs=[pl.BlockSpec((1, 1, S, 2 * MOBA_DH), lambda b, h: (b, h, 0, 0)),
                   pl.BlockSpec((1, 1, S, 2 * MOBA_DH), lambda b, h: (b, h, 0, 0)),
                   pl.BlockSpec((1, 1, S, MOBA_DH), lambda b, h: (b, h, 0, 0))],
        out_shape=[jax.ShapeDtypeStruct((B, MOBA_HEADS, S, 2 * MOBA_DH), BF16),
                   jax.ShapeDtypeStruct((B, MOBA_HEADS, S, 2 * MOBA_DH), BF16),
                   jax.ShapeDtypeStruct((B, MOBA_HEADS, S, MOBA_DH), BF16)],
        compiler_params=_params("arbitrary", "arbitrary"),
        name="moba_select",
    )(z3, z3, z3)


def _moba_flash_tile(qi, q_ref, k_ref, v_ref, bias_ref, o_ref, s_ref, p_ref):
    q = q_ref[0, 0]
    half = MOBA_BLOCK // 2
    m_wide = jnp.full((MOBA_BLOCK, half), -jnp.inf, F32)
    for j in range(qi + 1):
        lo = j * MOBA_BLOCK
        s = lax.dot_general(q, k_ref[0, 0, lo:lo + MOBA_BLOCK, :], NT_DIMS, preferred_element_type=F32)
        if j == qi:
            s = s + bias_ref[0, 0]
        elif j == qi - 1:
            s = s + bias_ref[0, 1]
        s_ref[:, lo:lo + MOBA_BLOCK] = s
        m_wide = jnp.maximum(m_wide, jnp.maximum(s[:, :half], s[:, half:]))
    m = jnp.broadcast_to(jnp.max(m_wide, axis=1, keepdims=True), (MOBA_BLOCK, half))
    l_wide = jnp.zeros((MOBA_BLOCK, half), F32)
    for j in range(qi + 1):
        lo = j * MOBA_BLOCK
        p_lo = jnp.exp2(s_ref[:, lo:lo + half] - m)
        p_hi = jnp.exp2(s_ref[:, lo + half:lo + MOBA_BLOCK] - m)
        l_wide = l_wide + (p_lo + p_hi)
        p_ref[:, lo:lo + half] = p_lo.astype(BF16)
        p_ref[:, lo + half:lo + MOBA_BLOCK] = p_hi.astype(BF16)
    n = (qi + 1) * MOBA_BLOCK
    acc = jnp.dot(p_ref[:, :n], v_ref[0, 0, :n, :], preferred_element_type=F32)
    o_ref[0] = (acc / jnp.sum(l_wide, axis=1, keepdims=True)).astype(BF16)


def _moba_flash_kernel(q_ref, k_ref, v_ref, bias_ref, o_ref, s_ref, p_ref, *, nb):
    i = pl.program_id(2)
    for qi in range(nb):
        @pl.when(i == qi)
        def _(qi=qi):
            _moba_flash_tile(qi, q_ref, k_ref, v_ref, bias_ref, o_ref, s_ref, p_ref)


def moba_flash(q_aug, k_aug, v, bias):
    B, H, S, _ = q_aug.shape
    nb = S // MOBA_BLOCK
    return pl.pallas_call(
        functools.partial(_moba_flash_kernel, nb=nb),
        grid=(B, H, nb),
        in_specs=[pl.BlockSpec((1, 1, MOBA_BLOCK, 2 * MOBA_DH), lambda b, h, i: (b, h, i, 0)),
                  pl.BlockSpec((1, 1, S, 2 * MOBA_DH), lambda b, h, i: (b, h, 0, 0)),
                  pl.BlockSpec((1, 1, S, MOBA_DH), lambda b, h, i: (b, h, 0, 0)),
                  pl.BlockSpec((1, 2, MOBA_BLOCK, MOBA_BLOCK), lambda b, h, i: (h, 0, 0, 0))],
        out_specs=pl.BlockSpec((1, MOBA_BLOCK, MOBA_DH), lambda b, h, i: (b, i, h)),
        out_shape=jax.ShapeDtypeStruct((B, S, MOBA_W), BF16),
        scratch_shapes=[pltpu.VMEM((MOBA_BLOCK, S), F32), pltpu.VMEM((MOBA_BLOCK, S), BF16)],
        compiler_params=_params("arbitrary", "arbitrary", "arbitrary"),
        name="moba_flash",
    )(q_aug, k_aug, v, bias)


def _page_sum_kernel(c_ref, o_ref):
    o_ref[...] = jnp.sum(c_ref[...], axis=1)


def page_sums(cache):
    P, page, H, dh = cache.shape
    g = _tile(P, 8)
    return pl.pallas_call(
        _page_sum_kernel,
        grid=(P // g,),
        in_specs=[pl.BlockSpec((g, page, H, dh), lambda i: (i, 0, 0, 0))],
        out_specs=pl.BlockSpec((g, H, dh), lambda i: (i, 0, 0)),
        out_shape=jax.ShapeDtypeStruct((P, H, dh), F32),
        compiler_params=_params("arbitrary"),
        name="page_sums",
    )(cache)


def _sample_select_kernel(pt_ref, ps_ref, q_ref, sel_ref, km_ref, *, nb_past, pages_per_block, past_len):
    n = pl.program_id(0)
    km_ref[...] = jnp.zeros(km_ref.shape, F32)

    def block_mean(b, _):
        acc = jnp.zeros((MOBA_HEADS, MOBA_DH), F32)
        for p in range(pages_per_block):
            acc = acc + ps_ref[0, pt_ref[n, b * pages_per_block + p]]
        acc = acc * (1.0 / MOBA_BLOCK)
        for h in range(MOBA_HEADS):
            km_ref[h, pl.ds(b, 1), :] = acc[h:h + 1, :]
        return 0

    lax.fori_loop(0, nb_past, block_mean, 0)

    rows = q_ref.shape[1]
    blk = lax.broadcasted_iota(jnp.int32, (rows, LANES), 1)
    pos = past_len + lax.broadcasted_iota(jnp.int32, (rows, LANES), 0)
    own = lax.shift_right_logical(pos, int(math.log2(MOBA_BLOCK)))
    for h in range(MOBA_HEADS):
        q = q_ref[0, :, h * MOBA_DH:(h + 1) * MOBA_DH]
        km = km_ref[h]
        gate = lax.dot_general(q, km, NT_DIMS, precision=lax.Precision.HIGHEST, preferred_element_type=F32)
        picks = _top_blocks(gate, blk, (blk < own) & (blk < nb_past))
        out = jnp.full((rows, LANES), -1, jnp.int32)
        for r, idx in enumerate(picks):
            out = jnp.where(blk == r, idx, out)
        sel_ref[0, h] = out


def sample_select(page_table, psums, layer, z3, past_len, page_size):
    Ns, rows, _ = z3.shape
    n_phys = psums.shape[1]
    pages_per_block = MOBA_BLOCK // page_size
    nb_past = past_len // MOBA_BLOCK
    assert nb_past <= LANES
    kern = functools.partial(_sample_select_kernel, nb_past=nb_past, pages_per_block=pages_per_block,
                             past_len=past_len)
    return pl.pallas_call(
        kern,
        grid=(Ns,),
        in_specs=[pl.BlockSpec(memory_space=pltpu.SMEM),
                  pl.BlockSpec((1, n_phys, MOBA_HEADS, MOBA_DH), lambda n: (layer, 0, 0, 0)),
                  pl.BlockSpec((1, rows, MOBA_W), lambda n: (n, 0, COL_QB // MOBA_W))],
        out_specs=pl.BlockSpec((1, MOBA_HEADS, rows, LANES), lambda n: (n, 0, 0, 0)),
        out_shape=jax.ShapeDtypeStruct((Ns, MOBA_HEADS, rows, LANES), jnp.int32),
        scratch_shapes=[pltpu.VMEM((MOBA_HEADS, LANES, MOBA_DH), F32)],
        compiler_params=_params("arbitrary"),
        name="sample_select",
    )(page_table, psums, z3)


def _sample_attn_kernel(pt_ref, q_ref, kn_ref, vn_ref, sel_ref, rb_ref, *rest,
                        t_true, past_len, page_size, n_pages, pages_per_step):
    pg = pages_per_step
    k_refs, v_refs = rest[:pg], rest[pg:2 * pg]
    o_ref, m_ref, l_ref, acc_ref, qa_ref, hm_ref, s_ref = rest[2 * pg:]
    j = pl.program_id(1)
    rows = q_ref.shape[1]
    ra = MOBA_HEADS * rows
    cols = page_size * MOBA_HEADS
    row_bits, head_bits = int(math.log2(rows)), int(math.log2(MOBA_HEADS))
    ppb = MOBA_BLOCK // page_size

    def heads_to_rows(ref):
        return jnp.concatenate([ref[0, :, h * MOBA_DH:(h + 1) * MOBA_DH] for h in range(MOBA_HEADS)], axis=0)

    def shifted_bias(dist):
        bucket = _t5_bucket(dist)
        return jnp.concatenate(
            [_bias_lookup(bucket[h * rows:(h + 1) * rows], rb_ref, h, rb_ref[REL_BUCKETS - 1, h])
             for h in range(MOBA_HEADS)], axis=0)

    @pl.when(j == 0)
    def _():
        qa = (heads_to_rows(q_ref) * (MOBA_DH ** -0.5)).astype(BF16)
        qa_ref[...] = qa
        r = lax.broadcasted_iota(jnp.int32, (ra, cols), 0)
        c = lax.broadcasted_iota(jnp.int32, (ra, cols), 1)
        hm_ref[...] = jnp.where(lax.shift_right_logical(r, row_bits) == (c & (MOBA_HEADS - 1)), 0.0, MASKED)
        kn = heads_to_rows(kn_ref).astype(BF16)
        vn = heads_to_rows(vn_ref).astype(BF16)
        s = lax.dot_general(qa, kn, NT_DIMS, preferred_element_type=F32)
        r = lax.broadcasted_iota(jnp.int32, (ra, ra), 0)
        c = lax.broadcasted_iota(jnp.int32, (ra, ra), 1)
        t, kt = r & (rows - 1), c & (rows - 1)
        valid = ((lax.shift_right_logical(r, row_bits) == lax.shift_right_logical(c, row_bits))
                 & (kt <= t) & (kt < t_true))
        s = jnp.where(valid, s + shifted_bias(t - kt), MASKED)
        m = jnp.max(s, axis=1, keepdims=True)
        p = jnp.exp(s - m)
        m_ref[...] = m
        l_ref[...] = jnp.sum(p, axis=1, keepdims=True)
        acc_ref[...] = jnp.dot(p.astype(BF16), vn, preferred_element_type=F32)

    sel = sel_ref[0].reshape(ra, LANES)
    lane = lax.broadcasted_iota(jnp.int32, (ra, LANES), 1)
    qa = qa_ref[...]
    n_near = -(-(REL_MAX_DIST - 1) // page_size)
    assert n_near <= pg
    m_wide = jnp.full((ra, LANES), -jnp.inf, F32)
    for r in range(pg):
        page = j * pg + r
        if r % ppb == 0:
            b = lax.shift_right_logical(page, int(math.log2(ppb)))
            chosen = jnp.max(jnp.where((sel == b) & (lane < MOBA_TOPK), 1.0, 0.0), axis=1, keepdims=True)
            row_mask = jnp.where(chosen > 0.0, 0.0, MASKED)
        s = lax.dot_general(qa, k_refs[r][0].astype(BF16), NT_DIMS, preferred_element_type=F32)
        s_ref[r] = s + hm_ref[...] + row_mask
        if r >= pg - n_near:
            @pl.when(j == pl.num_programs(1) - 1)
            def _(r=r):
                rr = lax.broadcasted_iota(jnp.int32, (ra, cols), 0)
                cc = lax.broadcasted_iota(jnp.int32, (ra, cols), 1)
                kpos = (n_pages - pg + r) * page_size + lax.shift_right_logical(cc, head_bits)
                s_ref[r] = s_ref[r] + shifted_bias(past_len + (rr & (rows - 1)) - kpos)
        s = s_ref[r]
        for c0 in range(0, cols, LANES):
            m_wide = jnp.maximum(m_wide, s[:, c0:c0 + LANES])
    m_old = m_ref[...]
    m_new = jnp.maximum(m_old, jnp.max(m_wide, axis=1, keepdims=True))
    a = jnp.exp(m_old - m_new)
    mb = jnp.broadcast_to(m_new, (ra, cols))
    acc = a * acc_ref[...]
    l_wide = jnp.zeros((ra, LANES), F32)
    for r in range(pg):
        p = jnp.exp(s_ref[r] - mb)
        for c0 in range(0, cols, LANES):
            l_wide = l_wide + p[:, c0:c0 + LANES]
        acc = acc + jnp.dot(p.astype(BF16), v_refs[r][0].astype(BF16), preferred_element_type=F32)
    m_ref[...] = m_new
    l_ref[...] = a * l_ref[...] + jnp.sum(l_wide, axis=1, keepdims=True)
    acc_ref[...] = acc

    @pl.when(j == pl.num_programs(1) - 1)
    def _():
        out = acc_ref[...] / l_ref[...]
        for h in range(MOBA_HEADS):
            o_ref[0, :, h * MOBA_DH:(h + 1) * MOBA_DH] = out[h * rows:(h + 1) * rows]


def sample_attention(page_table, sel, z3, cache_k, cache_v, layer, rel_bias, t_true, past_len):
    Ns, rows, _ = z3.shape
    n_pages = page_table.shape[1]
    page_size = cache_k.shape[2] // MOBA_HEADS
    cols = page_size * MOBA_HEADS
    ra = MOBA_HEADS * rows
    pg = SAMPLE_PAGES_PER_STEP
    assert n_pages % pg == 0 and pg % (MOBA_BLOCK // page_size) == 0 and past_len == n_pages * page_size
    kern = functools.partial(_sample_attn_kernel, t_true=t_true, past_len=past_len, page_size=page_size,
                             n_pages=n_pages, pages_per_step=pg)

    def page_spec(r):
        return pl.BlockSpec((None, 1, cols, MOBA_DH),
                            lambda n, j, pt: (layer, pt[n * n_pages + j * pg + r], 0, 0))

    def z_spec(col):
        return pl.BlockSpec((1, rows, MOBA_W), lambda n, j, pt: (n, 0, col // MOBA_W))

    grid_spec = pltpu.PrefetchScalarGridSpec(
        num_scalar_prefetch=1,
        grid=(Ns, n_pages // pg),
        in_specs=([z_spec(COL_QB), z_spec(COL_KB), z_spec(COL_VB),
                   pl.BlockSpec((1, MOBA_HEADS, rows, LANES), lambda n, j, pt: (n, 0, 0, 0)),
                   pl.BlockSpec(memory_space=pltpu.SMEM)]
                  + [page_spec(r) for r in range(pg)] + [page_spec(r) for r in range(pg)]),
        out_specs=pl.BlockSpec((1, rows, MOBA_W), lambda n, j, pt: (n, 0, 0)),
        scratch_shapes=[pltpu.VMEM((ra, 1), F32), pltpu.VMEM((ra, 1), F32), pltpu.VMEM((ra, MOBA_DH), F32),
                        pltpu.VMEM((ra, MOBA_DH), BF16), pltpu.VMEM((ra, cols), F32),
                        pltpu.VMEM((pg, ra, cols), F32)],
    )
    return pl.pallas_call(
        kern,
        grid_spec=grid_spec,
        out_shape=jax.ShapeDtypeStruct((Ns, rows, MOBA_W), F32),
        compiler_params=_params("arbitrary", "arbitrary"),
        name="sample_attention",
    )(page_table.reshape(-1), z3, z3, z3, sel, rel_bias, *([cache_k] * pg), *([cache_v] * pg))


def _layer_tail(x, z, y_a, moba_o, w, layer):
    m = merge_gates(y_a, moba_o, z, w["w_pa"], w["w_pb"], layer)
    x = out_proj(m, w["w_o"], x, layer)
    return ffn(x, w["norm2_w"], w["w_ff1"], w["w_ff2"], layer)


def kernel(x_prompt, x_sample, cache_k, cache_v, state_ret, page_table, rel_bias, norm1_w, w_in, ret_norm_w,
           w_pa, w_pb, w_o, norm2_w, w_ff1, w_ff2, final_norm_w):
    depth = w_in.shape[0]
    B, S, D = x_prompt.shape
    Ns, Ts, _ = x_sample.shape
    n_phys, page_size = cache_k.shape[1], cache_k.shape[2]
    n_pages = page_table.shape[1]
    past_len = n_pages * page_size
    assert D == D_MODEL and S % MOBA_BLOCK == 0 and past_len % MOBA_BLOCK == 0 and Ts <= SAMPLE_ROWS

    w = dict(w_pa=w_pa.astype(BF16), w_pb=w_pb.astype(BF16), w_o=w_o.astype(BF16), norm2_w=norm2_w,
             w_ff1=w_ff1.astype(BF16), w_ff2=w_ff2.astype(BF16))
    w_in = w_in.astype(BF16)

    ret_chunk = _tile(S, 256)
    cos_p, sin_p = _rope_tables(np.arange(S))
    bias_tables = moba_bias_tables(rel_bias)
    zero_state = jnp.zeros((1, B, RET_HEADS, RET_DK, RET_DV), F32)
    xp = x_prompt.reshape(B * S, D)
    pk, pv, ps = [], [], []
    for l in range(depth):
        z = in_proj(xp, norm1_w, w_in, l)
        z3 = z.reshape(B, S, D_IN)
        y_a, s_fin = retention(z3, cos_p, sin_p, ret_norm_w, l, zero_state, 0, ret_chunk, ret_chunk)
        q_aug, k_aug, v_b = moba_select(z3)
        moba_o = moba_flash(q_aug, k_aug, v_b, bias_tables)
        xp = _layer_tail(xp, z, y_a.reshape(B * S, RET_V_W), moba_o.reshape(B * S, MOBA_W), w, l)
        pk.append(z3[:, :, COL_KB:COL_KB + MOBA_W].reshape(B, S, MOBA_HEADS, MOBA_DH))
        pv.append(z3[:, :, COL_VB:COL_VB + MOBA_W].reshape(B, S, MOBA_HEADS, MOBA_DH))
        ps.append(s_fin)
    y_prompt = final_norm(xp, final_norm_w).reshape(B, S, D)

    R = SAMPLE_ROWS
    cos_s, sin_s = _rope_tables(past_len + np.arange(R))
    ck = cache_k.reshape(depth, n_phys, page_size * MOBA_HEADS, MOBA_DH)
    cv = cache_v.reshape(depth, n_phys, page_size * MOBA_HEADS, MOBA_DH)
    psums = page_sums(cache_k.reshape(depth * n_phys, page_size, MOBA_HEADS, MOBA_DH))
    psums = psums.reshape(depth, n_phys, MOBA_HEADS, MOBA_DH)
    xs = jnp.zeros((Ns, R, D), F32).at[:, :Ts].set(x_sample).reshape(Ns * R, D)
    sk, sv, ss = [], [], []
    for l in range(depth):
        z = in_proj(xs, norm1_w, w_in, l)
        z3 = z.reshape(Ns, R, D_IN)
        y_a, s_new = retention(z3, cos_s, sin_s, ret_norm_w, l, state_ret, l, R, Ts)
        sel = sample_select(page_table, psums, l, z3, past_len, page_size)
        moba_o = sample_attention(page_table, sel, z3, ck, cv, l, rel_bias, Ts, past_len)
        xs = _layer_tail(xs, z, y_a.reshape(Ns * R, RET_V_W), moba_o.reshape(Ns * R, MOBA_W), w, l)
        sk.append(z3[:, :Ts, COL_KB:COL_KB + MOBA_W].reshape(Ns, Ts, MOBA_HEADS, MOBA_DH))
        sv.append(z3[:, :Ts, COL_VB:COL_VB + MOBA_W].reshape(Ns, Ts, MOBA_HEADS, MOBA_DH))
        ss.append(s_new)
    y_sample = final_norm(xs, final_norm_w).reshape(Ns, R, D)[:, :Ts]

    return (y_prompt, y_sample, jnp.stack(pk), jnp.stack(pv), jnp.stack(ps),
            jnp.stack(sk), jnp.stack(sv), jnp.stack(ss))
```

```python
import functools
import math

import numpy as np
import jax
import jax.numpy as jnp
from jax import lax
from jax.experimental import pallas as pl
from jax.experimental.pallas import tpu as pltpu

F32 = jnp.float32
BF16 = jnp.bfloat16

D_MODEL = 2048
RET_HEADS = 8
RET_DK = 128
RET_DV = 256
MOBA_HEADS = 8
MOBA_DH = 128
MOBA_BLOCK = 256
MOBA_TOPK = 3
REL_BUCKETS = 32
REL_MAX_DIST = 128
ROPE_BASE = 10000.0
NORM_EPS = 1e-6

RET_QK_W = RET_HEADS * RET_DK
RET_V_W = RET_HEADS * RET_DV
MOBA_W = MOBA_HEADS * MOBA_DH
COL_QA = 0
COL_KA = COL_QA + RET_QK_W
COL_VA = COL_KA + RET_QK_W
COL_GA = COL_VA + RET_V_W
COL_QB = COL_GA + RET_V_W
COL_KB = COL_QB + MOBA_W
COL_VB = COL_KB + MOBA_W
COL_GATE_A = COL_VB + MOBA_W
COL_GATE_B = COL_GATE_A + D_MODEL
D_IN = COL_GATE_B + D_MODEL

LANES = 128
SAMPLE_ROWS = 16
SAMPLE_ATTN_ROWS = 8
SAMPLE_PAGES_PER_STEP = 8
MASKED = -1e9
LOG2_E = math.log2(math.e)
VMEM_LIMIT = 48 * 1024 * 1024

LOG_GAMMA = tuple(math.log(1.0 - 2.0 ** (-5.0 - h)) for h in range(RET_HEADS))

NT_DIMS = (((1,), (1,)), ((), ()))
TN_DIMS = (((0,), (0,)), ((), ()))


def _params(*sem):
    return pltpu.CompilerParams(dimension_semantics=sem, vmem_limit_bytes=VMEM_LIMIT)


def _tile(n, pref):
    t = min(n, pref)
    while n % t:
        t -= 1
    return t


def _rms(x, w):
    return x * lax.rsqrt(jnp.mean(x * x, axis=-1, keepdims=True) + NORM_EPS) * w


def _in_proj_kernel(x_ref, nw_ref, w_ref, o_ref, h_ref):
    @pl.when(pl.program_id(1) == 0)
    def _():
        h_ref[...] = _rms(x_ref[...], nw_ref[...]).astype(BF16)

    o_ref[...] = jnp.dot(h_ref[...], w_ref[...], preferred_element_type=F32)


def _layer_vec(w):
    return w.reshape(w.shape[0], 1, w.shape[1])


def in_proj(x, norm_w, w, layer):
    M, K = x.shape
    N = w.shape[2]
    tm, tn = _tile(M, 1024), _tile(N, 1024)
    return pl.pallas_call(
        _in_proj_kernel,
        grid=(M // tm, N // tn),
        in_specs=[pl.BlockSpec((tm, K), lambda i, j: (i, 0)),
                  pl.BlockSpec((None, 1, K), lambda i, j: (layer, 0, 0)),
                  pl.BlockSpec((None, K, tn), lambda i, j: (layer, 0, j))],
        out_specs=pl.BlockSpec((tm, tn), lambda i, j: (i, j)),
        out_shape=jax.ShapeDtypeStruct((M, N), F32),
        scratch_shapes=[pltpu.VMEM((tm, K), BF16)],
        compiler_params=_params("arbitrary", "arbitrary"),
        name="in_proj",
    )(x, _layer_vec(norm_w), w)


def _merge_kernel(ya_ref, mo_ref, ga_ref, gb_ref, wpa_ref, wpb_ref, o_ref):
    ua = jnp.dot(ya_ref[...].astype(BF16), wpa_ref[...], preferred_element_type=F32)
    ub = jnp.dot(mo_ref[...].astype(BF16), wpb_ref[...], preferred_element_type=F32)
    o_ref[...] = (jax.nn.sigmoid(ga_ref[...]) * ua + jax.nn.sigmoid(gb_ref[...]) * ub).astype(BF16)


def merge_gates(y_a, moba_o, z, w_pa, w_pb, layer):
    M = y_a.shape[0]
    tm, tn = _tile(M, 1024), 512
    ga0, gb0 = COL_GATE_A // tn, COL_GATE_B // tn
    return pl.pallas_call(
        _merge_kernel,
        grid=(M // tm, D_MODEL // tn),
        in_specs=[pl.BlockSpec((tm, RET_V_W), lambda i, j: (i, 0)),
                  pl.BlockSpec((tm, MOBA_W), lambda i, j: (i, 0)),
                  pl.BlockSpec((tm, tn), lambda i, j: (i, ga0 + j)),
                  pl.BlockSpec((tm, tn), lambda i, j: (i, gb0 + j)),
                  pl.BlockSpec((None, RET_V_W, tn), lambda i, j: (layer, 0, j)),
                  pl.BlockSpec((None, MOBA_W, tn), lambda i, j: (layer, 0, j))],
        out_specs=pl.BlockSpec((tm, tn), lambda i, j: (i, j)),
        out_shape=jax.ShapeDtypeStruct((M, D_MODEL), BF16),
        compiler_params=_params("arbitrary", "arbitrary"),
        name="merge_gates",
    )(y_a, moba_o, z, z, w_pa, w_pb)


def _out_proj_kernel(m_ref, w_ref, x_ref, o_ref):
    o_ref[...] = x_ref[...] + jnp.dot(m_ref[...], w_ref[...], preferred_element_type=F32)


def out_proj(m, w_o, x, layer):
    M, K = m.shape
    tm, tn = _tile(M, 1024), 512
    return pl.pallas_call(
        _out_proj_kernel,
        grid=(M // tm, D_MODEL // tn),
        in_specs=[pl.BlockSpec((tm, K), lambda i, j: (i, 0)),
                  pl.BlockSpec((None, K, tn), lambda i, j: (layer, 0, j)),
                  pl.BlockSpec((tm, tn), lambda i, j: (i, j))],
        out_specs=pl.BlockSpec((tm, tn), lambda i, j: (i, j)),
        out_shape=jax.ShapeDtypeStruct((M, D_MODEL), F32),
        compiler_params=_params("arbitrary", "arbitrary"),
        name="out_proj",
    )(m, w_o, x)


def _ffn_kernel(x_ref, nw_ref, w1_ref, w2_ref, o_ref, h_ref):
    @pl.when(pl.program_id(1) == 0)
    def _():
        x = x_ref[...]
        h_ref[...] = _rms(x, nw_ref[...]).astype(BF16)
        o_ref[...] = x

    a = jnp.dot(h_ref[...], w1_ref[...], preferred_element_type=F32)
    a = jnp.square(jnp.maximum(a, 0.0)).astype(BF16)
    o_ref[...] += jnp.dot(a, w2_ref[...], preferred_element_type=F32)


def ffn(x, norm_w, w1, w2, layer):
    M, D = x.shape
    FF = w1.shape[2]
    tm, tf = _tile(M, 512), _tile(FF, 1024)
    return pl.pallas_call(
        _ffn_kernel,
        grid=(M // tm, FF // tf),
        in_specs=[pl.BlockSpec((tm, D), lambda i, f: (i, 0)),
                  pl.BlockSpec((None, 1, D), lambda i, f: (layer, 0, 0)),
                  pl.BlockSpec((None, D, tf), lambda i, f: (layer, 0, f)),
                  pl.BlockSpec((None, tf, D), lambda i, f: (layer, f, 0))],
        out_specs=pl.BlockSpec((tm, D), lambda i, f: (i, 0)),
        out_shape=jax.ShapeDtypeStruct((M, D), F32),
        scratch_shapes=[pltpu.VMEM((tm, D), BF16)],
        compiler_params=_params("arbitrary", "arbitrary"),
        name="ffn",
    )(x, _layer_vec(norm_w), w1, w2)


def _final_norm_kernel(x_ref, nw_ref, o_ref):
    o_ref[...] = _rms(x_ref[...], nw_ref[...])


def final_norm(x, norm_w):
    M, D = x.shape
    tm = _tile(M, 1024)
    return pl.pallas_call(
        _final_norm_kernel,
        grid=(M // tm,),
        in_specs=[pl.BlockSpec((tm, D), lambda i: (i, 0)),
                  pl.BlockSpec((1, D), lambda i: (0, 0))],
        out_specs=pl.BlockSpec((tm, D), lambda i: (i, 0)),
        out_shape=jax.ShapeDtypeStruct((M, D), F32),
        compiler_params=_params("arbitrary"),
        name="final_norm",
    )(x, norm_w.reshape(1, D))


def _rope_tables(pos):
    half = RET_DK // 2
    inv = ROPE_BASE ** (-np.arange(half, dtype=np.float64) / half)
    ang = np.asarray(pos, np.float64)[:, None] * inv[None, :]
    cos = np.concatenate([np.cos(ang), np.cos(ang)], axis=1)
    sin = np.concatenate([-np.sin(ang), np.sin(ang)], axis=1)
    return jnp.asarray(cos, F32), jnp.asarray(sin, F32)


def _retention_kernel(q_ref, k_ref, v_ref, g_ref, cos_ref, sin_ref, nw_ref, s0_ref, y_ref, s_ref, dm_ref,
                      *, c_rows, c_true):
    c = pl.program_id(1)

    @pl.when(c == 0)
    def _():
        s_ref[...] = s0_ref[...]

    @pl.when((pl.program_id(0) == 0) & (c == 0))
    def _():
        i = lax.broadcasted_iota(jnp.int32, (c_rows, c_rows), 0)
        j = lax.broadcasted_iota(jnp.int32, (c_rows, c_rows), 1)
        diff = (i - j).astype(F32)
        for h in range(RET_HEADS):
            dm_ref[h] = jnp.where(diff >= 0, jnp.exp(LOG_GAMMA[h] * jnp.maximum(diff, 0.0)), 0.0)

    cos = cos_ref[...]
    sin = sin_ref[...]
    row = lax.broadcasted_iota(jnp.int32, (c_rows, 1), 0).astype(F32)
    for h in range(RET_HEADS):
        lg = LOG_GAMMA[h]
        q = q_ref[0, :, h * RET_DK:(h + 1) * RET_DK]
        k = k_ref[0, :, h * RET_DK:(h + 1) * RET_DK]
        v = v_ref[0, :, h * RET_DV:(h + 1) * RET_DV].astype(BF16)
        q = q * cos + pltpu.roll(q, RET_DK // 2, 1) * sin
        k = (k * cos + pltpu.roll(k, RET_DK // 2, 1) * sin) * (RET_DK ** -0.5)
        qb = q.astype(BF16)
        scores = lax.dot_general(qb, k.astype(BF16), NT_DIMS, preferred_element_type=F32) * dm_ref[h]
        o = jnp.dot(scores.astype(BF16), v, preferred_element_type=F32)
        s = s_ref[0, h]
        q_dec = jnp.exp(lg * (row + 1.0))
        o = o + jnp.dot(qb, s.astype(BF16), preferred_element_type=F32) * q_dec
        k_dec = jnp.exp(lg * (c_true - 1.0 - row))
        kd = k * k_dec
        if c_true < c_rows:
            kd = jnp.where(row < c_true, kd, 0.0)
        s_ref[0, h] = math.exp(lg * c_true) * s + lax.dot_general(
            kd.astype(BF16), v, TN_DIMS, preferred_element_type=F32)
        y = _rms(o, nw_ref[:, h * RET_DV:(h + 1) * RET_DV])
        g = g_ref[0, :, h * RET_DV:(h + 1) * RET_DV]
        y_ref[0, :, h * RET_DV:(h + 1) * RET_DV] = (y * (g * jax.nn.sigmoid(g))).astype(BF16)


def retention(z3, cos, sin, ret_norm_w, layer, s0, s0_layer, c_rows, c_true):
    N, T, _ = z3.shape
    nc = T // c_rows
    kern = functools.partial(_retention_kernel, c_rows=c_rows, c_true=c_true)
    return pl.pallas_call(
        kern,
        grid=(N, nc),
        in_specs=[pl.BlockSpec((1, c_rows, RET_QK_W), lambda n, c: (n, c, COL_QA // RET_QK_W)),
                  pl.BlockSpec((1, c_rows, RET_QK_W), lambda n, c: (n, c, COL_KA // RET_QK_W)),
                  pl.BlockSpec((1, c_rows, RET_V_W), lambda n, c: (n, c, COL_VA // RET_V_W)),
                  pl.BlockSpec((1, c_rows, RET_V_W), lambda n, c: (n, c, COL_GA // RET_V_W)),
                  pl.BlockSpec((c_rows, RET_DK), lambda n, c: (c, 0)),
                  pl.BlockSpec((c_rows, RET_DK), lambda n, c: (c, 0)),
                  pl.BlockSpec((None, 1, RET_V_W), lambda n, c: (layer, 0, 0)),
                  pl.BlockSpec((None, 1, RET_HEADS, RET_DK, RET_DV), lambda n, c: (s0_layer, n, 0, 0, 0))],
        out_specs=[pl.BlockSpec((1, c_rows, RET_V_W), lambda n, c: (n, c, 0)),
                   pl.BlockSpec((1, RET_HEADS, RET_DK, RET_DV), lambda n, c: (n, 0, 0, 0))],
        out_shape=[jax.ShapeDtypeStruct((N, T, RET_V_W), BF16),
                   jax.ShapeDtypeStruct((N, RET_HEADS, RET_DK, RET_DV), F32)],
        scratch_shapes=[pltpu.VMEM((RET_HEADS, c_rows, c_rows), F32)],
        compiler_params=_params("arbitrary", "arbitrary"),
        name="retention",
    )(z3, z3, z3, z3, cos, sin, _layer_vec(ret_norm_w), s0)


def _t5_bucket_np(dist):
    n = np.maximum(dist, 0)
    max_exact = REL_BUCKETS // 2
    nf = np.maximum(n, 1).astype(np.float64)
    large = max_exact + (np.log(nf / max_exact) / math.log(REL_MAX_DIST / max_exact)
                         * (REL_BUCKETS - max_exact)).astype(np.int64)
    large = np.minimum(large, REL_BUCKETS - 1)
    return np.where(n < max_exact, n, large).astype(np.int32)


def _t5_bucket(dist):
    n = jnp.maximum(dist, 0)
    max_exact = REL_BUCKETS // 2
    nf = jnp.maximum(n, 1).astype(F32)
    large = max_exact + (jnp.log(nf / max_exact) / math.log(REL_MAX_DIST / max_exact)
                         * (REL_BUCKETS - max_exact)).astype(jnp.int32)
    large = jnp.minimum(large, REL_BUCKETS - 1)
    return jnp.where(n < max_exact, n, large)


def _bias_lookup(bucket, rb_ref, h, shift):
    out = jnp.zeros(bucket.shape, F32)
    for b in range(REL_BUCKETS):
        out = jnp.where(bucket == b, rb_ref[b, h] - shift, out)
    return out


def _top_blocks(gate, blk, eligible):
    g = jnp.where(eligible, gate, -jnp.inf)
    picks = []
    for _ in range(MOBA_TOPK):
        m = jnp.max(g, axis=1, keepdims=True)
        cand = (g == m) & (m > -jnp.inf)
        idx = jnp.min(jnp.where(cand, blk, LANES), axis=1, keepdims=True)
        idx = jnp.where(idx == LANES, -1, idx)
        picks.append(idx)
        g = jnp.where(blk == idx, -jnp.inf, g)
    return picks


def _moba_bias_kernel(rb_ref, bk_ref, o_ref):
    h = pl.program_id(0)
    far = rb_ref[REL_BUCKETS - 1, h]
    i = lax.broadcasted_iota(jnp.int32, (MOBA_BLOCK, MOBA_BLOCK), 0)
    j = lax.broadcasted_iota(jnp.int32, (MOBA_BLOCK, MOBA_BLOCK), 1)
    o_ref[0, 0] = jnp.where(i >= j, _bias_lookup(bk_ref[0], rb_ref, h, far) * LOG2_E, MASKED)
    o_ref[0, 1] = _bias_lookup(bk_ref[1], rb_ref, h, far) * LOG2_E


def moba_bias_tables(rel_bias):
    d = np.arange(MOBA_BLOCK)[:, None] - np.arange(MOBA_BLOCK)[None, :]
    buckets = jnp.asarray(np.stack([_t5_bucket_np(d), _t5_bucket_np(d + MOBA_BLOCK)]))
    return pl.pallas_call(
        _moba_bias_kernel,
        grid=(MOBA_HEADS,),
        in_specs=[pl.BlockSpec(memory_space=pltpu.SMEM),
                  pl.BlockSpec((2, MOBA_BLOCK, MOBA_BLOCK), lambda h: (0, 0, 0))],
        out_specs=pl.BlockSpec((1, 2, MOBA_BLOCK, MOBA_BLOCK), lambda h: (h, 0, 0, 0)),
        out_shape=jax.ShapeDtypeStruct((MOBA_HEADS, 2, MOBA_BLOCK, MOBA_BLOCK), F32),
        compiler_params=_params("arbitrary"),
        name="moba_bias_tables",
    )(rel_bias, buckets)


def _moba_select_kernel(q_ref, k_ref, v_ref, qa_ref, ka_ref, vb_ref, *, nb):
    q = q_ref[0]
    k = k_ref[0]
    S = q.shape[0]
    kmean = jnp.mean(k.reshape(nb, MOBA_BLOCK, MOBA_DH), axis=1)
    if nb < LANES:
        kmean = jnp.concatenate([kmean, jnp.zeros((LANES - nb, MOBA_DH), F32)], axis=0)
    gate = lax.dot_general(q, kmean, NT_DIMS, precision=lax.Precision.HIGHEST, preferred_element_type=F32)
    blk = lax.broadcasted_iota(jnp.int32, (S, LANES), 1)
    own = lax.shift_right_logical(lax.broadcasted_iota(jnp.int32, (S, LANES), 0), int(math.log2(MOBA_BLOCK)))
    picks = _top_blocks(gate, blk, blk < own)
    allowed = blk == own
    for idx in picks:
        allowed = allowed | (blk == idx)
    qa_ref[0, 0, :, :MOBA_DH] = (q * (MOBA_DH ** -0.5 * LOG2_E)).astype(BF16)
    qa_ref[0, 0, :, MOBA_DH:] = jnp.where(allowed, 0.0, MASKED).astype(BF16)
    ka_ref[0, 0, :, :MOBA_DH] = k.astype(BF16)
    ka_ref[0, 0, :, MOBA_DH:] = jnp.where(blk == own, 1.0, 0.0).astype(BF16)
    vb_ref[0, 0] = v_ref[0].astype(BF16)


def moba_select(z3):
    B, S, _ = z3.shape
    nb = S // MOBA_BLOCK
    kern = functools.partial(_moba_select_kernel, nb=nb)
    return pl.pallas_call(
        kern,
        grid=(B, MOBA_HEADS),
        in_specs=[pl.BlockSpec((1, S, MOBA_DH), lambda b, h: (b, 0, COL_QB // MOBA_DH + h)),
                  pl.BlockSpec((1, S, MOBA_DH), lambda b, h: (b, 0, COL_KB // MOBA_DH + h)),
                  pl.BlockSpec((1, S, MOBA_DH), lambda b, h: (b, 0, COL_VB // MOBA_DH + h))],
        out_specs=[pl.BlockSpec((1, 1, S, 2 * MOBA_DH), lambda b, h: (b, h, 0, 0)),
                   pl.BlockSpec((1, 1, S, 2 * MOBA_DH), lambda b, h: (b, h, 0, 0)),
                   pl.BlockSpec((1, 1, S, MOBA_DH), lambda b, h: (b, h, 0, 0))],
        out_shape=[jax.ShapeDtypeStruct((B, MOBA_HEADS, S, 2 * MOBA_DH), BF16),
                   jax.ShapeDtypeStruct((B, MOBA_HEADS, S, 2 * MOBA_DH), BF16),
                   jax.ShapeDtypeStruct((B, MOBA_HEADS, S, MOBA_DH), BF16)],
        compiler_params=_params("arbitrary", "arbitrary"),
        name="moba_select",
    )(z3, z3, z3)


def _moba_flash_tile(qi, q_ref, k_ref, v_ref, bias_ref, o_ref, s_ref, p_ref):
    q = q_ref[0, 0]
    half = MOBA_BLOCK // 2
    m_wide = jnp.full((MOBA_BLOCK, half), -jnp.inf, F32)
    for j in range(qi + 1):
        lo = j * MOBA_BLOCK
        s = lax.dot_general(q, k_ref[0, 0, lo:lo + MOBA_BLOCK, :], NT_DIMS, preferred_element_type=F32)
        if j == qi:
            s = s + bias_ref[0, 0]
        elif j == qi - 1:
            s = s + bias_ref[0, 1]
        s_ref[:, lo:lo + MOBA_BLOCK] = s
        m_wide = jnp.maximum(m_wide, jnp.maximum(s[:, :half], s[:, half:]))
    m = jnp.broadcast_to(jnp.max(m_wide, axis=1, keepdims=True), (MOBA_BLOCK, half))
    l_wide = jnp.zeros((MOBA_BLOCK, half), F32)
    for j in range(qi + 1):
        lo = j * MOBA_BLOCK
        p_lo = jnp.exp2(s_ref[:, lo:lo + half] - m)
        p_hi = jnp.exp2(s_ref[:, lo + half:lo + MOBA_BLOCK] - m)
        l_wide = l_wide + (p_lo + p_hi)
        p_ref[:, lo:lo + half] = p_lo.astype(BF16)
        p_ref[:, lo + half:lo + MOBA_BLOCK] = p_hi.astype(BF16)
    n = (qi + 1) * MOBA_BLOCK
    acc = jnp.dot(p_ref[:, :n], v_ref[0, 0, :n, :], preferred_element_type=F32)
    o_ref[0] = (acc / jnp.sum(l_wide, axis=1, keepdims=True)).astype(BF16)


def _moba_flash_kernel(q_ref, k_ref, v_ref, bias_ref, o_ref, s_ref, p_ref, *, nb):
    i = pl.program_id(2)
    for qi in range(nb):
        @pl.when(i == qi)
        def _(qi=qi):
            _moba_flash_tile(qi, q_ref, k_ref, v_ref, bias_ref, o_ref, s_ref, p_ref)


def moba_flash(q_aug, k_aug, v, bias):
    B, H, S, _ = q_aug.shape
    nb = S // MOBA_BLOCK
    return pl.pallas_call(
        functools.partial(_moba_flash_kernel, nb=nb),
        grid=(B, H, nb),
        in_specs=[pl.BlockSpec((1, 1, MOBA_BLOCK, 2 * MOBA_DH), lambda b, h, i: (b, h, i, 0)),
                  pl.BlockSpec((1, 1, S, 2 * MOBA_DH), lambda b, h, i: (b, h, 0, 0)),
                  pl.BlockSpec((1, 1, S, MOBA_DH), lambda b, h, i: (b, h, 0, 0)),
                  pl.BlockSpec((1, 2, MOBA_BLOCK, MOBA_BLOCK), lambda b, h, i: (h, 0, 0, 0))],
        out_specs=pl.BlockSpec((1, MOBA_BLOCK, MOBA_DH), lambda b, h, i: (b, i, h)),
        out_shape=jax.ShapeDtypeStruct((B, S, MOBA_W), BF16),
        scratch_shapes=[pltpu.VMEM((MOBA_BLOCK, S), F32), pltpu.VMEM((MOBA_BLOCK, S), BF16)],
        compiler_params=_params("arbitrary", "arbitrary", "arbitrary"),
        name="moba_flash",
    )(q_aug, k_aug, v, bias)


def _page_sum_kernel(c_ref, o_ref):
    o_ref[...] = jnp.sum(c_ref[...], axis=1)


def page_sums(cache):
    P, page, H, dh = cache.shape
    g = _tile(P, 8)
    return pl.pallas_call(
        _page_sum_kernel,
        grid=(P // g,),
        in_specs=[pl.BlockSpec((g, page, H, dh), lambda i: (i, 0, 0, 0))],
        out_specs=pl.BlockSpec((g, H, dh), lambda i: (i, 0, 0)),
        out_shape=jax.ShapeDtypeStruct((P, H, dh), F32),
        compiler_params=_params("arbitrary"),
        name="page_sums",
    )(cache)


def _sample_select_kernel(pt_ref, ps_ref, q_ref, sel_ref, km_ref, *, nb_past, pages_per_block, past_len):
    n = pl.program_id(0)
    km_ref[...] = jnp.zeros(km_ref.shape, F32)

    def block_mean(b, _):
        acc = jnp.zeros((MOBA_HEADS, MOBA_DH), F32)
        for p in range(pages_per_block):
            acc = acc + ps_ref[0, pt_ref[n, b * pages_per_block + p]]
        acc = acc * (1.0 / MOBA_BLOCK)
        for h in range(MOBA_HEADS):
            km_ref[h, pl.ds(b, 1), :] = acc[h:h + 1, :]
        return 0

    lax.fori_loop(0, nb_past, block_mean, 0)

    rows = q_ref.shape[1]
    blk = lax.broadcasted_iota(jnp.int32, (rows, LANES), 1)
    pos = past_len + lax.broadcasted_iota(jnp.int32, (rows, LANES), 0)
    own = lax.shift_right_logical(pos, int(math.log2(MOBA_BLOCK)))
    for h in range(MOBA_HEADS):
        q = q_ref[0, :, h * MOBA_DH:(h + 1) * MOBA_DH]
        km = km_ref[h]
        gate = lax.dot_general(q, km, NT_DIMS, precision=lax.Precision.HIGHEST, preferred_element_type=F32)
        picks = _top_blocks(gate, blk, (blk < own) & (blk < nb_past))
        out = jnp.full((rows, LANES), -1, jnp.int32)
        for r, idx in enumerate(picks):
            out = jnp.where(blk == r, idx, out)
        sel_ref[0, h] = out


def sample_select(page_table, psums, layer, z3, past_len, page_size):
    Ns, rows, _ = z3.shape
    n_phys = psums.shape[1]
    pages_per_block = MOBA_BLOCK // page_size
    nb_past = past_len // MOBA_BLOCK
    assert nb_past <= LANES
    kern = functools.partial(_sample_select_kernel, nb_past=nb_past, pages_per_block=pages_per_block,
                             past_len=past_len)
    return pl.pallas_call(
        kern,
        grid=(Ns,),
        in_specs=[pl.BlockSpec(memory_space=pltpu.SMEM),
                  pl.BlockSpec((1, n_phys, MOBA_HEADS, MOBA_DH), lambda n: (layer, 0, 0, 0)),
                  pl.BlockSpec((1, rows, MOBA_W), lambda n: (n, 0, COL_QB // MOBA_W))],
        out_specs=pl.BlockSpec((1, MOBA_HEADS, rows, LANES), lambda n: (n, 0, 0, 0)),
        out_shape=jax.ShapeDtypeStruct((Ns, MOBA_HEADS, rows, LANES), jnp.int32),
        scratch_shapes=[pltpu.VMEM((MOBA_HEADS, LANES, MOBA_DH), F32)],
        compiler_params=_params("arbitrary"),
        name="sample_select",
    )(page_table, psums, z3)


def _sample_attn_kernel(pt_ref, q_ref, kn_ref, vn_ref, sel_ref, rb_ref, *rest,
                        t_true, past_len, page_size, n_pages, pages_per_step):
    pg = pages_per_step
    k_refs, v_refs = rest[:pg], rest[pg:2 * pg]
    o_ref, m_ref, l_ref, acc_ref, qa_ref, hm_ref, s_ref = rest[2 * pg:]
    j = pl.program_id(1)
    rows = SAMPLE_ATTN_ROWS
    ra = MOBA_HEADS * rows
    cols = page_size * MOBA_HEADS
    row_bits, head_bits = int(math.log2(rows)), int(math.log2(MOBA_HEADS))
    ppb = MOBA_BLOCK // page_size

    def heads_to_rows(ref):
        return jnp.concatenate([ref[0, :rows, h * MOBA_DH:(h + 1) * MOBA_DH] for h in range(MOBA_HEADS)], axis=0)

    def shifted_bias(dist):
        bucket = _t5_bucket(dist)
        return jnp.concatenate(
            [_bias_lookup(bucket[h * rows:(h + 1) * rows], rb_ref, h, rb_ref[REL_BUCKETS - 1, h])
             for h in range(MOBA_HEADS)], axis=0)

    @pl.when(j == 0)
    def _():
        qa = (heads_to_rows(q_ref) * (MOBA_DH ** -0.5)).astype(BF16)
        qa_ref[...] = qa
        r = lax.broadcasted_iota(jnp.int32, (ra, cols), 0)
        c = lax.broadcasted_iota(jnp.int32, (ra, cols), 1)
        hm_ref[...] = jnp.where(lax.shift_right_logical(r, row_bits) == (c & (MOBA_HEADS - 1)), 0.0, MASKED)
        kn = heads_to_rows(kn_ref).astype(BF16)
        vn = heads_to_rows(vn_ref).astype(BF16)
        s = lax.dot_general(qa, kn, NT_DIMS, preferred_element_type=F32)
        r = lax.broadcasted_iota(jnp.int32, (ra, ra), 0)
        c = lax.broadcasted_iota(jnp.int32, (ra, ra), 1)
        t, kt = r & (rows - 1), c & (rows - 1)
        valid = ((lax.shift_right_logical(r, row_bits) == lax.shift_right_logical(c, row_bits))
                 & (kt <= t) & (kt < t_true))
        s = jnp.where(valid, s + shifted_bias(t - kt), MASKED)
        m = jnp.max(s, axis=1, keepdims=True)
        p = jnp.exp(s - m)
        m_ref[...] = m
        l_ref[...] = jnp.sum(p, axis=1, keepdims=True)
        acc_ref[...] = jnp.dot(p.astype(BF16), vn, preferred_element_type=F32)

    sel = sel_ref[0, :, :rows, :].reshape(ra, LANES)
    lane = lax.broadcasted_iota(jnp.int32, (ra, LANES), 1)
    qa = qa_ref[...]
    n_near = -(-(REL_MAX_DIST - 1) // page_size)
    assert n_near <= pg
    m_wide = jnp.full((ra, LANES), -jnp.inf, F32)
    for r in range(pg):
        page = j * pg + r
        if r % ppb == 0:
            b = lax.shift_right_logical(page, int(math.log2(ppb)))
            chosen = jnp.max(jnp.where((sel == b) & (lane < MOBA_TOPK), 1.0, 0.0), axis=1, keepdims=True)
            row_mask = jnp.where(chosen > 0.0, 0.0, MASKED)
        s = lax.dot_general(qa, k_refs[r][0].astype(BF16), NT_DIMS, preferred_element_type=F32)
        s_ref[r] = s + hm_ref[...] + row_mask
        if r >= pg - n_near:
            @pl.when(j == pl.num_programs(1) - 1)
            def _(r=r):
                rr = lax.broadcasted_iota(jnp.int32, (ra, cols), 0)
                cc = lax.broadcasted_iota(jnp.int32, (ra, cols), 1)
                kpos = (n_pages - pg + r) * page_size + lax.shift_right_logical(cc, head_bits)
                s_ref[r] = s_ref[r] + shifted_bias(past_len + (rr & (rows - 1)) - kpos)
        s = s_ref[r]
        for c0 in range(0, cols, LANES):
            m_wide = jnp.maximum(m_wide, s[:, c0:c0 + LANES])
    m_old = m_ref[...]
    m_new = jnp.maximum(m_old, jnp.max(m_wide, axis=1, keepdims=True))
    a = jnp.exp(m_old - m_new)
    mb = jnp.broadcast_to(m_new, (ra, cols))
    acc = a * acc_ref[...]
    l_wide = jnp.zeros((ra, LANES), F32)
    for r in range(pg):
        p = jnp.exp(s_ref[r] - mb)
        for c0 in range(0, cols, LANES):
            l_wide = l_wide + p[:, c0:c0 + LANES]
        acc = acc + jnp.dot(p.astype(BF16), v_refs[r][0].astype(BF16), preferred_element_type=F32)
    m_ref[...] = m_new
    l_ref[...] = a * l_ref[...] + jnp.sum(l_wide, axis=1, keepdims=True)
    acc_ref[...] = acc

    @pl.when(j == pl.num_programs(1) - 1)
    def _():
        out = acc_ref[...] / l_ref[...]
        o_ref[...] = jnp.zeros(o_ref.shape, F32)
        for h in range(MOBA_HEADS):
            o_ref[0, :rows, h * MOBA_DH:(h + 1) * MOBA_DH] = out[h * rows:(h + 1) * rows]


def sample_attention(page_table, sel, z3, cache_k, cache_v, layer, rel_bias, t_true, past_len):
    Ns, rows, _ = z3.shape
    n_pages = page_table.shape[1]
    page_size = cache_k.shape[2] // MOBA_HEADS
    cols = page_size * MOBA_HEADS
    ra = MOBA_HEADS * SAMPLE_ATTN_ROWS
    pg = SAMPLE_PAGES_PER_STEP
    assert t_true <= SAMPLE_ATTN_ROWS <= rows
    assert n_pages % pg == 0 and pg % (MOBA_BLOCK // page_size) == 0 and past_len == n_pages * page_size
    kern = functools.partial(_sample_attn_kernel, t_true=t_true, past_len=past_len, page_size=page_size,
                             n_pages=n_pages, pages_per_step=pg)

    def page_spec(r):
        return pl.BlockSpec((None, 1, cols, MOBA_DH),
                            lambda n, j, pt: (layer, pt[n * n_pages + j * pg + r], 0, 0))

    def z_spec(col):
        return pl.BlockSpec((1, rows, MOBA_W), lambda n, j, pt: (n, 0, col // MOBA_W))

    grid_spec = pltpu.PrefetchScalarGridSpec(
        num_scalar_prefetch=1,
        grid=(Ns, n_pages // pg),
        in_specs=([z_spec(COL_QB), z_spec(COL_KB), z_spec(COL_VB),
                   pl.BlockSpec((1, MOBA_HEADS, rows, LANES), lambda n, j, pt: (n, 0, 0, 0)),
                   pl.BlockSpec(memory_space=pltpu.SMEM)]
                  + [page_spec(r) for r in range(pg)] + [page_spec(r) for r in range(pg)]),
        out_specs=pl.BlockSpec((1, rows, MOBA_W), lambda n, j, pt: (n, 0, 0)),
        scratch_shapes=[pltpu.VMEM((ra, 1), F32), pltpu.VMEM((ra, 1), F32), pltpu.VMEM((ra, MOBA_DH), F32),
                        pltpu.VMEM((ra, MOBA_DH), BF16), pltpu.VMEM((ra, cols), F32),
                        pltpu.VMEM((pg, ra, cols), F32)],
    )
    return pl.pallas_call(
        kern,
        grid_spec=grid_spec,
        out_shape=jax.ShapeDtypeStruct((Ns, rows, MOBA_W), F32),
        compiler_params=_params("arbitrary", "arbitrary"),
        name="sample_attention",
    )(page_table.reshape(-1), z3, z3, z3, sel, rel_bias, *([cache_k] * pg), *([cache_v] * pg))


def _sample_rows(ref, rows):
    return jnp.concatenate([ref[0, :rows, h * MOBA_DH:(h + 1) * MOBA_DH] for h in range(MOBA_HEADS)], axis=0)


def _sample_shifted_bias(dist, rb_ref, rows):
    bucket = _t5_bucket(dist)
    return jnp.concatenate(
        [_bias_lookup(bucket[h * rows:(h + 1) * rows], rb_ref, h, rb_ref[REL_BUCKETS - 1, h])
         for h in range(MOBA_HEADS)], axis=0)


def _sample_blocks_kernel(pt_ref, q_ref, rb_ref, *rest, past_len, page_size, n_pages, pages_per_step):
    pg = pages_per_step
    k_refs, v_refs = rest[:pg], rest[pg:2 * pg]
    acc_ref, m_ref, l_ref, g_ref, qa_ref, qf_ref, hm_ref, s_ref = rest[2 * pg:]
    j = pl.program_id(1)
    rows = SAMPLE_ATTN_ROWS
    ra = MOBA_HEADS * rows
    cols = page_size * MOBA_HEADS
    row_bits, head_bits = int(math.log2(rows)), int(math.log2(MOBA_HEADS))
    ppb = MOBA_BLOCK // page_size
    bps = pg // ppb

    @pl.when(j == 0)
    def _():
        qf = _sample_rows(q_ref, rows) * (MOBA_DH ** -0.5)
        qf_ref[...] = qf
        qa_ref[...] = qf.astype(BF16)
        r = lax.broadcasted_iota(jnp.int32, (ra, cols), 0)
        c = lax.broadcasted_iota(jnp.int32, (ra, cols), 1)
        hm_ref[...] = jnp.where(lax.shift_right_logical(r, row_bits) == (c & (MOBA_HEADS - 1)), 0.0, MASKED)
        m_ref[...] = jnp.zeros(m_ref.shape, F32)
        l_ref[...] = jnp.zeros(l_ref.shape, F32)
        g_ref[...] = jnp.zeros(g_ref.shape, F32)

    qa = qa_ref[...]
    qf = qf_ref[...]
    lane = lax.broadcasted_iota(jnp.int32, (ra, LANES), 1)
    m_all, l_all, g_all = m_ref[0], l_ref[0], g_ref[0]
    n_near = -(-(REL_MAX_DIST - 1) // page_size)
    assert n_near <= ppb
    for bb in range(bps):
        m_wide = jnp.full((ra, LANES), -jnp.inf, F32)
        ksum = jnp.zeros((MOBA_HEADS, MOBA_DH), F32)
        for p in range(ppb):
            r = bb * ppb + p
            k = k_refs[r][0]
            ksum = ksum + jnp.sum(k.reshape(page_size, MOBA_HEADS, MOBA_DH), axis=0)
            s = lax.dot_general(qa, k.astype(BF16), NT_DIMS, preferred_element_type=F32)
            s_ref[p] = s + hm_ref[...]
            if r >= pg - n_near:
                @pl.when(j == pl.num_programs(1) - 1)
                def _(r=r, p=p):
                    rr = lax.broadcasted_iota(jnp.int32, (ra, cols), 0)
                    cc = lax.broadcasted_iota(jnp.int32, (ra, cols), 1)
                    kpos = (n_pages - pg + r) * page_size + lax.shift_right_logical(cc, head_bits)
                    s_ref[p] = s_ref[p] + _sample_shifted_bias(past_len + (rr & (rows - 1)) - kpos, rb_ref, rows)
            s = s_ref[p]
            for c0 in range(0, cols, LANES):
                m_wide = jnp.maximum(m_wide, s[:, c0:c0 + LANES])
        m_b = jnp.max(m_wide, axis=1, keepdims=True)
        mb = jnp.broadcast_to(m_b, (ra, cols))
        l_wide = jnp.zeros((ra, LANES), F32)
        acc = jnp.zeros((ra, MOBA_DH), F32)
        for p in range(ppb):
            e = jnp.exp(s_ref[p] - mb)
            for c0 in range(0, cols, LANES):
                l_wide = l_wide + e[:, c0:c0 + LANES]
            acc = acc + jnp.dot(e.astype(BF16), v_refs[bb * ppb + p][0].astype(BF16), preferred_element_type=F32)
        acc_ref[0, bb] = acc
        kmean = ksum * (1.0 / MOBA_BLOCK)
        kmean_rows = jnp.concatenate(
            [jnp.broadcast_to(kmean[h:h + 1, :], (rows, MOBA_DH)) for h in range(MOBA_HEADS)], axis=0)
        gate = jnp.sum(qf * kmean_rows, axis=1, keepdims=True)
        here = lane == j * bps + bb
        m_all = jnp.where(here, m_b, m_all)
        l_all = jnp.where(here, jnp.sum(l_wide, axis=1, keepdims=True), l_all)
        g_all = jnp.where(here, gate, g_all)
    m_ref[0], l_ref[0], g_ref[0] = m_all, l_all, g_all


def sample_blocks(page_table, z3, cache_k, cache_v, layer, rel_bias, past_len):
    Ns, R, _ = z3.shape
    n_pages = page_table.shape[1]
    page_size = cache_k.shape[2] // MOBA_HEADS
    cols = page_size * MOBA_HEADS
    ra = MOBA_HEADS * SAMPLE_ATTN_ROWS
    pg = SAMPLE_PAGES_PER_STEP
    ppb = MOBA_BLOCK // page_size
    nb = n_pages // ppb
    assert n_pages % pg == 0 and pg % ppb == 0 and past_len == n_pages * page_size and nb <= LANES
    kern = functools.partial(_sample_blocks_kernel, past_len=past_len, page_size=page_size, n_pages=n_pages,
                             pages_per_step=pg)

    def page_spec(r):
        return pl.BlockSpec((None, 1, cols, MOBA_DH),
                            lambda n, j, pt: (layer, pt[n * n_pages + j * pg + r], 0, 0))

    stat_spec = pl.BlockSpec((1, ra, LANES), lambda n, j, pt: (n, 0, 0))
    stat_shape = jax.ShapeDtypeStruct((Ns, ra, LANES), F32)
    grid_spec = pltpu.PrefetchScalarGridSpec(
        num_scalar_prefetch=1,
        grid=(Ns, n_pages // pg),
        in_specs=([pl.BlockSpec((1, R, MOBA_W), lambda n, j, pt: (n, 0, COL_QB // MOBA_W)),
                   pl.BlockSpec(memory_space=pltpu.SMEM)]
                  + [page_spec(r) for r in range(pg)] + [page_spec(r) for r in range(pg)]),
        out_specs=[pl.BlockSpec((1, pg // ppb, ra, MOBA_DH), lambda n, j, pt: (n, j, 0, 0)),
                   stat_spec, stat_spec, stat_spec],
        scratch_shapes=[pltpu.VMEM((ra, MOBA_DH), BF16), pltpu.VMEM((ra, MOBA_DH), F32),
                        pltpu.VMEM((ra, cols), F32), pltpu.VMEM((ppb, ra, cols), F32)],
    )
    return pl.pallas_call(
        kern,
        grid_spec=grid_spec,
        out_shape=[jax.ShapeDtypeStruct((Ns, nb, ra, MOBA_DH), F32), stat_shape, stat_shape, stat_shape],
        compiler_params=_params("arbitrary", "arbitrary"),
        name="sample_blocks",
    )(page_table.reshape(-1), z3, rel_bias, *([cache_k] * pg), *([cache_v] * pg))


def _sample_combine_kernel(q_ref, kn_ref, vn_ref, acc_ref, m_ref, l_ref, g_ref, rb_ref, o_ref,
                           *, t_true, past_len, nb_past):
    rows = SAMPLE_ATTN_ROWS
    ra = MOBA_HEADS * rows
    row_bits = int(math.log2(rows))
    qa = (_sample_rows(q_ref, rows) * (MOBA_DH ** -0.5)).astype(BF16)
    kn = _sample_rows(kn_ref, rows).astype(BF16)
    vn = _sample_rows(vn_ref, rows).astype(BF16)
    s = lax.dot_general(qa, kn, NT_DIMS, preferred_element_type=F32)
    r = lax.broadcasted_iota(jnp.int32, (ra, ra), 0)
    c = lax.broadcasted_iota(jnp.int32, (ra, ra), 1)
    t, kt = r & (rows - 1), c & (rows - 1)
    valid = ((lax.shift_right_logical(r, row_bits) == lax.shift_right_logical(c, row_bits))
             & (kt <= t) & (kt < t_true))
    s = jnp.where(valid, s + _sample_shifted_bias(t - kt, rb_ref, rows), MASKED)
    m_own = jnp.max(s, axis=1, keepdims=True)
    e = jnp.exp(s - m_own)
    l_own = jnp.sum(e, axis=1, keepdims=True)
    acc_own = jnp.dot(e.astype(BF16), vn, preferred_element_type=F32)

    lane = lax.broadcasted_iota(jnp.int32, (ra, LANES), 1)
    pos = past_len + (lax.broadcasted_iota(jnp.int32, (ra, LANES), 0) & (rows - 1))
    own = lax.shift_right_logical(pos, int(math.log2(MOBA_BLOCK)))
    picks = _top_blocks(g_ref[0], lane, (lane < own) & (lane < nb_past))
    chosen = lane == picks[0]
    for idx in picks[1:]:
        chosen = chosen | (lane == idx)
    m_blk = jnp.where(chosen, m_ref[0], -jnp.inf)
    m_tot = jnp.maximum(m_own, jnp.max(m_blk, axis=1, keepdims=True))
    w_blk = jnp.exp(m_blk - m_tot)
    w_own = jnp.exp(m_own - m_tot)
    l_tot = w_own * l_own + jnp.sum(w_blk * l_ref[0], axis=1, keepdims=True)
    acc = w_own * acc_own
    for b in range(nb_past):
        acc = acc + w_blk[:, b:b + 1] * acc_ref[0, b]
    out = acc / l_tot
    o_ref[...] = jnp.zeros(o_ref.shape, F32)
    for h in range(MOBA_HEADS):
        o_ref[0, :rows, h * MOBA_DH:(h + 1) * MOBA_DH] = out[h * rows:(h + 1) * rows]


def sample_combine(z3, acc, m, l, g, rel_bias, t_true, past_len):
    Ns, R, _ = z3.shape
    nb_past, ra = acc.shape[1], acc.shape[2]
    assert t_true <= SAMPLE_ATTN_ROWS <= R
    kern = functools.partial(_sample_combine_kernel, t_true=t_true, past_len=past_len, nb_past=nb_past)

    def z_spec(col):
        return pl.BlockSpec((1, R, MOBA_W), lambda n: (n, 0, col // MOBA_W))

    stat_spec = pl.BlockSpec((1, ra, LANES), lambda n: (n, 0, 0))
    return pl.pallas_call(
        kern,
        grid=(Ns,),
        in_specs=[z_spec(COL_QB), z_spec(COL_KB), z_spec(COL_VB),
                  pl.BlockSpec((1, nb_past, ra, MOBA_DH), lambda n: (n, 0, 0, 0)),
                  stat_spec, stat_spec, stat_spec,
                  pl.BlockSpec(memory_space=pltpu.SMEM)],
        out_specs=pl.BlockSpec((1, R, MOBA_W), lambda n: (n, 0, 0)),
        out_shape=jax.ShapeDtypeStruct((Ns, R, MOBA_W), F32),
        compiler_params=_params("arbitrary"),
        name="sample_combine",
    )(z3, z3, z3, acc, m, l, g, rel_bias)


def _layer_tail(x, z, y_a, moba_o, w, layer):
    m = merge_gates(y_a, moba_o, z, w["w_pa"], w["w_pb"], layer)
    x = out_proj(m, w["w_o"], x, layer)
    return ffn(x, w["norm2_w"], w["w_ff1"], w["w_ff2"], layer)


def kernel(x_prompt, x_sample, cache_k, cache_v, state_ret, page_table, rel_bias, norm1_w, w_in, ret_norm_w,
           w_pa, w_pb, w_o, norm2_w, w_ff1, w_ff2, final_norm_w):
    depth = w_in.shape[0]
    B, S, D = x_prompt.shape
    Ns, Ts, _ = x_sample.shape
    n_phys, page_size = cache_k.shape[1], cache_k.shape[2]
    n_pages = page_table.shape[1]
    past_len = n_pages * page_size
    assert D == D_MODEL and S % MOBA_BLOCK == 0 and past_len % MOBA_BLOCK == 0 and Ts <= SAMPLE_ROWS

    w = dict(w_pa=w_pa.astype(BF16), w_pb=w_pb.astype(BF16), w_o=w_o.astype(BF16), norm2_w=norm2_w,
             w_ff1=w_ff1.astype(BF16), w_ff2=w_ff2.astype(BF16))
    w_in = w_in.astype(BF16)

    ret_chunk = _tile(S, 256)
    cos_p, sin_p = _rope_tables(np.arange(S))
    bias_tables = moba_bias_tables(rel_bias)
    zero_state = jnp.zeros((1, B, RET_HEADS, RET_DK, RET_DV), F32)
    xp = x_prompt.reshape(B * S, D)
    pk, pv, ps = [], [], []
    for l in range(depth):
        z = in_proj(xp, norm1_w, w_in, l)
        z3 = z.reshape(B, S, D_IN)
        y_a, s_fin = retention(z3, cos_p, sin_p, ret_norm_w, l, zero_state, 0, ret_chunk, ret_chunk)
        q_aug, k_aug, v_b = moba_select(z3)
        moba_o = moba_flash(q_aug, k_aug, v_b, bias_tables)
        xp = _layer_tail(xp, z, y_a.reshape(B * S, RET_V_W), moba_o.reshape(B * S, MOBA_W), w, l)
        pk.append(z3[:, :, COL_KB:COL_KB + MOBA_W].reshape(B, S, MOBA_HEADS, MOBA_DH))
        pv.append(z3[:, :, COL_VB:COL_VB + MOBA_W].reshape(B, S, MOBA_HEADS, MOBA_DH))
        ps.append(s_fin)
    y_prompt = final_norm(xp, final_norm_w).reshape(B, S, D)

    R = SAMPLE_ROWS
    cos_s, sin_s = _rope_tables(past_len + np.arange(R))
    ck = cache_k.reshape(depth, n_phys, page_size * MOBA_HEADS, MOBA_DH)
    cv = cache_v.reshape(depth, n_phys, page_size * MOBA_HEADS, MOBA_DH)
    xs = jnp.zeros((Ns, R, D), F32).at[:, :Ts].set(x_sample).reshape(Ns * R, D)
    sk, sv, ss = [], [], []
    for l in range(depth):
        z = in_proj(xs, norm1_w, w_in, l)
        z3 = z.reshape(Ns, R, D_IN)
        y_a, s_new = retention(z3, cos_s, sin_s, ret_norm_w, l, state_ret, l, R, Ts)
        blk_acc, blk_m, blk_l, blk_g = sample_blocks(page_table, z3, ck, cv, l, rel_bias, past_len)
        moba_o = sample_combine(z3, blk_acc, blk_m, blk_l, blk_g, rel_bias, Ts, past_len)
        xs = _layer_tail(xs, z, y_a.reshape(Ns * R, RET_V_W), moba_o.reshape(Ns * R, MOBA_W), w, l)
        sk.append(z3[:, :Ts, COL_KB:COL_KB + MOBA_W].reshape(Ns, Ts, MOBA_HEADS, MOBA_DH))
        sv.append(z3[:, :Ts, COL_VB:COL_VB + MOBA_W].reshape(Ns, Ts, MOBA_HEADS, MOBA_DH))
        ss.append(s_new)
    y_sample = final_norm(xs, final_norm_w).reshape(Ns, R, D)[:, :Ts]

    return (y_prompt, y_sample, jnp.stack(pk), jnp.stack(pv), jnp.stack(ps),
            jnp.stack(sk), jnp.stack(sv), jnp.stack(ss))
```

```python
import functools
import math

import numpy as np
import jax
import jax.numpy as jnp
from jax import lax
from jax.experimental import pallas as pl
from jax.experimental.pallas import tpu as pltpu

F32 = jnp.float32
BF16 = jnp.bfloat16

D_MODEL = 2048
RET_HEADS = 8
RET_DK = 128
RET_DV = 256
MOBA_HEADS = 8
MOBA_DH = 128
MOBA_BLOCK = 256
MOBA_TOPK = 3
REL_BUCKETS = 32
REL_MAX_DIST = 128
ROPE_BASE = 10000.0
NORM_EPS = 1e-6

RET_QK_W = RET_HEADS * RET_DK
RET_V_W = RET_HEADS * RET_DV
MOBA_W = MOBA_HEADS * MOBA_DH
COL_QA = 0
COL_KA = COL_QA + RET_QK_W
COL_VA = COL_KA + RET_QK_W
COL_GA = COL_VA + RET_V_W
COL_QB = COL_GA + RET_V_W
COL_KB = COL_QB + MOBA_W
COL_VB = COL_KB + MOBA_W
COL_GATE_A = COL_VB + MOBA_W
COL_GATE_B = COL_GATE_A + D_MODEL
D_IN = COL_GATE_B + D_MODEL

LANES = 128
SAMPLE_ROWS = 16
FLASH_STEPS = 1
SAMPLE_ATTN_ROWS = 8
SAMPLE_PAGES_PER_STEP = 8
MASKED = -1e9
LOG2_E = math.log2(math.e)
VMEM_LIMIT = 48 * 1024 * 1024

LOG_GAMMA = tuple(math.log(1.0 - 2.0 ** (-5.0 - h)) for h in range(RET_HEADS))

NT_DIMS = (((1,), (1,)), ((), ()))
TN_DIMS = (((0,), (0,)), ((), ()))


def _params(*sem):
    return pltpu.CompilerParams(dimension_semantics=sem, vmem_limit_bytes=VMEM_LIMIT)


def _tile(n, pref):
    t = min(n, pref)
    while n % t:
        t -= 1
    return t


def _rms(x, w):
    return x * lax.rsqrt(jnp.mean(x * x, axis=-1, keepdims=True) + NORM_EPS) * w


def _in_proj_kernel(x_ref, nw_ref, w_ref, o_ref, h_ref):
    @pl.when(pl.program_id(1) == 0)
    def _():
        h_ref[...] = _rms(x_ref[...], nw_ref[...]).astype(BF16)

    o_ref[...] = jnp.dot(h_ref[...], w_ref[...], preferred_element_type=F32)


def _layer_vec(w):
    return w.reshape(w.shape[0], 1, w.shape[1])


def in_proj(x, norm_w, w, layer):
    M, K = x.shape
    N = w.shape[2]
    tm, tn = _tile(M, 1024), _tile(N, 1024)
    return pl.pallas_call(
        _in_proj_kernel,
        grid=(M // tm, N // tn),
        in_specs=[pl.BlockSpec((tm, K), lambda i, j: (i, 0)),
                  pl.BlockSpec((None, 1, K), lambda i, j: (layer, 0, 0)),
                  pl.BlockSpec((None, K, tn), lambda i, j: (layer, 0, j))],
        out_specs=pl.BlockSpec((tm, tn), lambda i, j: (i, j)),
        out_shape=jax.ShapeDtypeStruct((M, N), F32),
        scratch_shapes=[pltpu.VMEM((tm, K), BF16)],
        compiler_params=_params("arbitrary", "arbitrary"),
        name="in_proj",
    )(x, _layer_vec(norm_w), w)


def _merge_kernel(ya_ref, mo_ref, ga_ref, gb_ref, wpa_ref, wpb_ref, o_ref):
    ua = jnp.dot(ya_ref[...].astype(BF16), wpa_ref[...], preferred_element_type=F32)
    ub = jnp.dot(mo_ref[...].astype(BF16), wpb_ref[...], preferred_element_type=F32)
    o_ref[...] = (jax.nn.sigmoid(ga_ref[...]) * ua + jax.nn.sigmoid(gb_ref[...]) * ub).astype(BF16)


def merge_gates(y_a, moba_o, z, w_pa, w_pb, layer):
    M = y_a.shape[0]
    tm, tn = _tile(M, 1024), 512
    ga0, gb0 = COL_GATE_A // tn, COL_GATE_B // tn
    return pl.pallas_call(
        _merge_kernel,
        grid=(M // tm, D_MODEL // tn),
        in_specs=[pl.BlockSpec((tm, RET_V_W), lambda i, j: (i, 0)),
                  pl.BlockSpec((tm, MOBA_W), lambda i, j: (i, 0)),
                  pl.BlockSpec((tm, tn), lambda i, j: (i, ga0 + j)),
                  pl.BlockSpec((tm, tn), lambda i, j: (i, gb0 + j)),
                  pl.BlockSpec((None, RET_V_W, tn), lambda i, j: (layer, 0, j)),
                  pl.BlockSpec((None, MOBA_W, tn), lambda i, j: (layer, 0, j))],
        out_specs=pl.BlockSpec((tm, tn), lambda i, j: (i, j)),
        out_shape=jax.ShapeDtypeStruct((M, D_MODEL), BF16),
        compiler_params=_params("arbitrary", "arbitrary"),
        name="merge_gates",
    )(y_a, moba_o, z, z, w_pa, w_pb)


def _out_proj_kernel(m_ref, w_ref, x_ref, o_ref):
    o_ref[...] = x_ref[...] + jnp.dot(m_ref[...], w_ref[...], preferred_element_type=F32)


def out_proj(m, w_o, x, layer):
    M, K = m.shape
    tm, tn = _tile(M, 1024), 1024
    return pl.pallas_call(
        _out_proj_kernel,
        grid=(M // tm, D_MODEL // tn),
        in_specs=[pl.BlockSpec((tm, K), lambda i, j: (i, 0)),
                  pl.BlockSpec((None, K, tn), lambda i, j: (layer, 0, j)),
                  pl.BlockSpec((tm, tn), lambda i, j: (i, j))],
        out_specs=pl.BlockSpec((tm, tn), lambda i, j: (i, j)),
        out_shape=jax.ShapeDtypeStruct((M, D_MODEL), F32),
        compiler_params=_params("arbitrary", "arbitrary"),
        name="out_proj",
    )(m, w_o, x)


def _ffn_kernel(x_ref, nw_ref, w1_ref, w2_ref, o_ref, h_ref):
    @pl.when(pl.program_id(1) == 0)
    def _():
        x = x_ref[...]
        h_ref[...] = _rms(x, nw_ref[...]).astype(BF16)
        o_ref[...] = x

    a = jnp.dot(h_ref[...], w1_ref[...], preferred_element_type=F32)
    a = jnp.square(jnp.maximum(a, 0.0)).astype(BF16)
    o_ref[...] += jnp.dot(a, w2_ref[...], preferred_element_type=F32)


def ffn(x, norm_w, w1, w2, layer):
    M, D = x.shape
    FF = w1.shape[2]
    tm, tf = _tile(M, 512), _tile(FF, 1024)
    return pl.pallas_call(
        _ffn_kernel,
        grid=(M // tm, FF // tf),
        in_specs=[pl.BlockSpec((tm, D), lambda i, f: (i, 0)),
                  pl.BlockSpec((None, 1, D), lambda i, f: (layer, 0, 0)),
                  pl.BlockSpec((None, D, tf), lambda i, f: (layer, 0, f)),
                  pl.BlockSpec((None, tf, D), lambda i, f: (layer, f, 0))],
        out_specs=pl.BlockSpec((tm, D), lambda i, f: (i, 0)),
        out_shape=jax.ShapeDtypeStruct((M, D), F32),
        scratch_shapes=[pltpu.VMEM((tm, D), BF16)],
        compiler_params=_params("arbitrary", "arbitrary"),
        name="ffn",
    )(x, _layer_vec(norm_w), w1, w2)


def _final_norm_kernel(x_ref, nw_ref, o_ref):
    o_ref[...] = _rms(x_ref[...], nw_ref[...])


def final_norm(x, norm_w):
    M, D = x.shape
    tm = _tile(M, 1024)
    return pl.pallas_call(
        _final_norm_kernel,
        grid=(M // tm,),
        in_specs=[pl.BlockSpec((tm, D), lambda i: (i, 0)),
                  pl.BlockSpec((1, D), lambda i: (0, 0))],
        out_specs=pl.BlockSpec((tm, D), lambda i: (i, 0)),
        out_shape=jax.ShapeDtypeStruct((M, D), F32),
        compiler_params=_params("arbitrary"),
        name="final_norm",
    )(x, norm_w.reshape(1, D))


def _rope_tables(pos):
    half = RET_DK // 2
    inv = ROPE_BASE ** (-np.arange(half, dtype=np.float64) / half)
    ang = np.asarray(pos, np.float64)[:, None] * inv[None, :]
    cos = np.concatenate([np.cos(ang), np.cos(ang)], axis=1)
    sin = np.concatenate([-np.sin(ang), np.sin(ang)], axis=1)
    return jnp.asarray(cos, F32), jnp.asarray(sin, F32)


def _retention_kernel(q_ref, k_ref, v_ref, g_ref, cos_ref, sin_ref, nw_ref, s0_ref, y_ref, s_ref, dm_ref,
                      *, c_rows, c_true):
    c = pl.program_id(1)

    @pl.when(c == 0)
    def _():
        s_ref[...] = s0_ref[...]

    @pl.when((pl.program_id(0) == 0) & (c == 0))
    def _():
        i = lax.broadcasted_iota(jnp.int32, (c_rows, c_rows), 0)
        j = lax.broadcasted_iota(jnp.int32, (c_rows, c_rows), 1)
        diff = (i - j).astype(F32)
        for h in range(RET_HEADS):
            dm_ref[h] = jnp.where(diff >= 0, jnp.exp(LOG_GAMMA[h] * jnp.maximum(diff, 0.0)), 0.0)

    cos = cos_ref[...]
    sin = sin_ref[...]
    row = lax.broadcasted_iota(jnp.int32, (c_rows, 1), 0).astype(F32)
    for h in range(RET_HEADS):
        lg = LOG_GAMMA[h]
        q = q_ref[0, :, h * RET_DK:(h + 1) * RET_DK]
        k = k_ref[0, :, h * RET_DK:(h + 1) * RET_DK]
        v = v_ref[0, :, h * RET_DV:(h + 1) * RET_DV].astype(BF16)
        q = q * cos + pltpu.roll(q, RET_DK // 2, 1) * sin
        k = (k * cos + pltpu.roll(k, RET_DK // 2, 1) * sin) * (RET_DK ** -0.5)
        qb = q.astype(BF16)
        scores = lax.dot_general(qb, k.astype(BF16), NT_DIMS, preferred_element_type=F32) * dm_ref[h]
        o = jnp.dot(scores.astype(BF16), v, preferred_element_type=F32)
        s = s_ref[0, h]
        q_dec = jnp.exp(lg * (row + 1.0))
        o = o + jnp.dot(qb, s.astype(BF16), preferred_element_type=F32) * q_dec
        k_dec = jnp.exp(lg * (c_true - 1.0 - row))
        kd = k * k_dec
        if c_true < c_rows:
            kd = jnp.where(row < c_true, kd, 0.0)
        s_ref[0, h] = math.exp(lg * c_true) * s + lax.dot_general(
            kd.astype(BF16), v, TN_DIMS, preferred_element_type=F32)
        y = _rms(o, nw_ref[:, h * RET_DV:(h + 1) * RET_DV])
        g = g_ref[0, :, h * RET_DV:(h + 1) * RET_DV]
        y_ref[0, :, h * RET_DV:(h + 1) * RET_DV] = (y * (g * jax.nn.sigmoid(g))).astype(BF16)


def retention(z3, cos, sin, ret_norm_w, layer, s0, s0_layer, c_rows, c_true):
    N, T, _ = z3.shape
    nc = T // c_rows
    kern = functools.partial(_retention_kernel, c_rows=c_rows, c_true=c_true)
    return pl.pallas_call(
        kern,
        grid=(N, nc),
        in_specs=[pl.BlockSpec((1, c_rows, RET_QK_W), lambda n, c: (n, c, COL_QA // RET_QK_W)),
                  pl.BlockSpec((1, c_rows, RET_QK_W), lambda n, c: (n, c, COL_KA // RET_QK_W)),
                  pl.BlockSpec((1, c_rows, RET_V_W), lambda n, c: (n, c, COL_VA // RET_V_W)),
                  pl.BlockSpec((1, c_rows, RET_V_W), lambda n, c: (n, c, COL_GA // RET_V_W)),
                  pl.BlockSpec((c_rows, RET_DK), lambda n, c: (c, 0)),
                  pl.BlockSpec((c_rows, RET_DK), lambda n, c: (c, 0)),
                  pl.BlockSpec((None, 1, RET_V_W), lambda n, c: (layer, 0, 0)),
                  pl.BlockSpec((None, 1, RET_HEADS, RET_DK, RET_DV), lambda n, c: (s0_layer, n, 0, 0, 0))],
        out_specs=[pl.BlockSpec((1, c_rows, RET_V_W), lambda n, c: (n, c, 0)),
                   pl.BlockSpec((1, RET_HEADS, RET_DK, RET_DV), lambda n, c: (n, 0, 0, 0))],
        out_shape=[jax.ShapeDtypeStruct((N, T, RET_V_W), BF16),
                   jax.ShapeDtypeStruct((N, RET_HEADS, RET_DK, RET_DV), F32)],
        scratch_shapes=[pltpu.VMEM((RET_HEADS, c_rows, c_rows), F32)],
        compiler_params=_params("arbitrary", "arbitrary"),
        name="retention",
    )(z3, z3, z3, z3, cos, sin, _layer_vec(ret_norm_w), s0)


def _t5_bucket_np(dist):
    n = np.maximum(dist, 0)
    max_exact = REL_BUCKETS // 2
    nf = np.maximum(n, 1).astype(np.float64)
    large = max_exact + (np.log(nf / max_exact) / math.log(REL_MAX_DIST / max_exact)
                         * (REL_BUCKETS - max_exact)).astype(np.int64)
    large = np.minimum(large, REL_BUCKETS - 1)
    return np.where(n < max_exact, n, large).astype(np.int32)


def _t5_bucket(dist):
    n = jnp.maximum(dist, 0)
    max_exact = REL_BUCKETS // 2
    nf = jnp.maximum(n, 1).astype(F32)
    large = max_exact + (jnp.log(nf / max_exact) / math.log(REL_MAX_DIST / max_exact)
                         * (REL_BUCKETS - max_exact)).astype(jnp.int32)
    large = jnp.minimum(large, REL_BUCKETS - 1)
    return jnp.where(n < max_exact, n, large)


def _bias_lookup(bucket, rb_ref, h, shift):
    out = jnp.zeros(bucket.shape, F32)
    for b in range(REL_BUCKETS):
        out = jnp.where(bucket == b, rb_ref[b, h] - shift, out)
    return out


def _top_blocks(gate, blk, eligible, axis=1):
    none = LANES
    g = jnp.where(eligible, gate, -jnp.inf)
    picks = []
    for _ in range(MOBA_TOPK):
        m = jnp.max(g, axis=axis, keepdims=True)
        cand = (g == m) & (m > -jnp.inf)
        idx = jnp.min(jnp.where(cand, blk, none), axis=axis, keepdims=True)
        idx = jnp.where(idx == none, -1, idx)
        picks.append(idx)
        g = jnp.where(blk == idx, -jnp.inf, g)
    return picks


def _moba_bias_kernel(rb_ref, bk_ref, o_ref):
    h = pl.program_id(0)
    far = rb_ref[REL_BUCKETS - 1, h]
    i = lax.broadcasted_iota(jnp.int32, (MOBA_BLOCK, MOBA_BLOCK), 0)
    j = lax.broadcasted_iota(jnp.int32, (MOBA_BLOCK, MOBA_BLOCK), 1)
    o_ref[0, 0] = jnp.where(i >= j, _bias_lookup(bk_ref[0], rb_ref, h, far) * LOG2_E, MASKED)
    o_ref[0, 1] = _bias_lookup(bk_ref[1], rb_ref, h, far) * LOG2_E


def moba_bias_tables(rel_bias):
    d = np.arange(MOBA_BLOCK)[:, None] - np.arange(MOBA_BLOCK)[None, :]
    buckets = jnp.asarray(np.stack([_t5_bucket_np(d), _t5_bucket_np(d + MOBA_BLOCK)]))
    return pl.pallas_call(
        _moba_bias_kernel,
        grid=(MOBA_HEADS,),
        in_specs=[pl.BlockSpec(memory_space=pltpu.SMEM),
                  pl.BlockSpec((2, MOBA_BLOCK, MOBA_BLOCK), lambda h: (0, 0, 0))],
        out_specs=pl.BlockSpec((1, 2, MOBA_BLOCK, MOBA_BLOCK), lambda h: (h, 0, 0, 0)),
        out_shape=jax.ShapeDtypeStruct((MOBA_HEADS, 2, MOBA_BLOCK, MOBA_BLOCK), F32),
        compiler_params=_params("arbitrary"),
        name="moba_bias_tables",
    )(rel_bias, buckets)


def _moba_select_kernel(q_ref, k_ref, v_ref, oh_ref, qa_ref, ka_ref, vb_ref, *, nb):
    q = q_ref[0]
    k = k_ref[0]
    S = q.shape[0]
    nbp = -(-nb // 8) * 8
    assert nbp <= LANES
    kmean = jnp.mean(k.reshape(nb, MOBA_BLOCK, MOBA_DH), axis=1)
    if nb < nbp:
        kmean = jnp.concatenate([kmean, jnp.zeros((nbp - nb, MOBA_DH), F32)], axis=0)
    gate = lax.dot_general(kmean, q, NT_DIMS, precision=lax.Precision.HIGHEST, preferred_element_type=F32)
    blk = lax.broadcasted_iota(jnp.int32, (nbp, S), 0)
    own = lax.shift_right_logical(lax.broadcasted_iota(jnp.int32, (nbp, S), 1), int(math.log2(MOBA_BLOCK)))
    blk_f = blk.astype(F32)
    picks = _top_blocks(gate, blk_f, blk < own, axis=0)
    allowed = blk == own
    for idx in picks:
        allowed = allowed | (blk_f == idx)
    eye = jnp.where(lax.broadcasted_iota(jnp.int32, (nbp, LANES), 0)
                    == lax.broadcasted_iota(jnp.int32, (nbp, LANES), 1), 1.0, 0.0).astype(BF16)
    allowed_q = lax.dot_general(jnp.where(allowed, 1.0, 0.0).astype(BF16), eye, TN_DIMS,
                                preferred_element_type=F32)
    qa_ref[0, 0, :, :MOBA_DH] = (q * (MOBA_DH ** -0.5 * LOG2_E)).astype(BF16)
    qa_ref[0, 0, :, MOBA_DH:] = ((1.0 - allowed_q) * MASKED).astype(BF16)
    ka_ref[0, 0, :, :MOBA_DH] = k.astype(BF16)
    ka_ref[0, 0, :, MOBA_DH:] = oh_ref[...]
    vb_ref[0, 0] = v_ref[0].astype(BF16)


def moba_select(z3):
    B, S, _ = z3.shape
    nb = S // MOBA_BLOCK
    kern = functools.partial(_moba_select_kernel, nb=nb)
    onehot = jnp.asarray(np.arange(S)[:, None] // MOBA_BLOCK == np.arange(LANES)[None, :], BF16)
    return pl.pallas_call(
        kern,
        grid=(B, MOBA_HEADS),
        in_specs=[pl.BlockSpec((1, S, MOBA_DH), lambda b, h: (b, 0, COL_QB // MOBA_DH + h)),
                  pl.BlockSpec((1, S, MOBA_DH), lambda b, h: (b, 0, COL_KB // MOBA_DH + h)),
                  pl.BlockSpec((1, S, MOBA_DH), lambda b, h: (b, 0, COL_VB // MOBA_DH + h)),
                  pl.BlockSpec((S, LANES), lambda b, h: (0, 0))],
        out_specs=[pl.BlockSpec((1, 1, S, 2 * MOBA_DH), lambda b, h: (b, h, 0, 0)),
                   pl.BlockSpec((1, 1, S, 2 * MOBA_DH), lambda b, h: (b, h, 0, 0)),
                   pl.BlockSpec((1, 1, S, MOBA_DH), lambda b, h: (b, h, 0, 0))],
        out_shape=[jax.ShapeDtypeStruct((B, MOBA_HEADS, S, 2 * MOBA_DH), BF16),
                   jax.ShapeDtypeStruct((B, MOBA_HEADS, S, 2 * MOBA_DH), BF16),
                   jax.ShapeDtypeStruct((B, MOBA_HEADS, S, MOBA_DH), BF16)],
        compiler_params=_params("arbitrary", "arbitrary"),
        name="moba_select",
    )(z3, z3, z3, onehot)


def _moba_flash_tile(qi, q_ref, k_ref, v_ref, bias_ref, o_ref, s_ref, p_ref):
    q = q_ref[0, 0, qi * MOBA_BLOCK:(qi + 1) * MOBA_BLOCK, :]
    half = MOBA_BLOCK // 2
    m_wide = jnp.full((MOBA_BLOCK, half), -jnp.inf, F32)
    for j in range(qi + 1):
        lo = j * MOBA_BLOCK
        s = lax.dot_general(q, k_ref[0, 0, lo:lo + MOBA_BLOCK, :], NT_DIMS, preferred_element_type=F32)
        if j == qi:
            s = s + bias_ref[0, 0]
        elif j == qi - 1:
            s = s + bias_ref[0, 1]
        s_ref[:, lo:lo + MOBA_BLOCK] = s
        m_wide = jnp.maximum(m_wide, jnp.maximum(s[:, :half], s[:, half:]))
    m = jnp.broadcast_to(jnp.max(m_wide, axis=1, keepdims=True), (MOBA_BLOCK, half))
    l_wide = jnp.zeros((MOBA_BLOCK, half), F32)
    for j in range(qi + 1):
        lo = j * MOBA_BLOCK
        p_lo = jnp.exp2(s_ref[:, lo:lo + half] - m)
        p_hi = jnp.exp2(s_ref[:, lo + half:lo + MOBA_BLOCK] - m)
        l_wide = l_wide + (p_lo + p_hi)
        p_ref[:, lo:lo + half] = p_lo.astype(BF16)
        p_ref[:, lo + half:lo + MOBA_BLOCK] = p_hi.astype(BF16)
    n = (qi + 1) * MOBA_BLOCK
    acc = jnp.dot(p_ref[:, :n], v_ref[0, 0, :n, :], preferred_element_type=F32)
    o_ref[0, qi * MOBA_BLOCK:(qi + 1) * MOBA_BLOCK, :] = (
        acc / jnp.sum(l_wide, axis=1, keepdims=True)).astype(BF16)


def _flash_groups(nb, steps):
    pairs = [(i, nb - 1 - i) for i in range(nb // 2)]
    return [sum(pairs[g::steps], ()) for g in range(steps)]


def _moba_flash_kernel(q_ref, k_ref, v_ref, bias_ref, o_ref, s_ref, p_ref, *, groups):
    i = pl.program_id(2)
    for g, blocks in enumerate(groups):
        @pl.when(i == g)
        def _(blocks=blocks):
            for qi in blocks:
                _moba_flash_tile(qi, q_ref, k_ref, v_ref, bias_ref, o_ref, s_ref, p_ref)


def moba_flash(q_aug, k_aug, v, bias):
    B, H, S, _ = q_aug.shape
    nb = S // MOBA_BLOCK
    steps = FLASH_STEPS if nb % (2 * FLASH_STEPS) == 0 else 1
    groups = _flash_groups(nb, steps) if nb % 2 == 0 else [tuple(range(nb))]
    return pl.pallas_call(
        functools.partial(_moba_flash_kernel, groups=groups),
        grid=(B, H, len(groups)),
        in_specs=[pl.BlockSpec((1, 1, S, 2 * MOBA_DH), lambda b, h, i: (b, h, 0, 0)),
                  pl.BlockSpec((1, 1, S, 2 * MOBA_DH), lambda b, h, i: (b, h, 0, 0)),
                  pl.BlockSpec((1, 1, S, MOBA_DH), lambda b, h, i: (b, h, 0, 0)),
                  pl.BlockSpec((1, 2, MOBA_BLOCK, MOBA_BLOCK), lambda b, h, i: (h, 0, 0, 0))],
        out_specs=pl.BlockSpec((1, S, MOBA_DH), lambda b, h, i: (b, 0, h)),
        out_shape=jax.ShapeDtypeStruct((B, S, MOBA_W), BF16),
        scratch_shapes=[pltpu.VMEM((MOBA_BLOCK, S), F32), pltpu.VMEM((MOBA_BLOCK, S), BF16)],
        compiler_params=_params("arbitrary", "arbitrary", "arbitrary"),
        name="moba_flash",
    )(q_aug, k_aug, v, bias)


def _sample_rows(ref, rows):
    return jnp.concatenate([ref[0, :rows, h * MOBA_DH:(h + 1) * MOBA_DH] for h in range(MOBA_HEADS)], axis=0)


def _sample_shifted_bias(dist, rb_ref, rows):
    bucket = _t5_bucket(dist)
    return jnp.concatenate(
        [_bias_lookup(bucket[h * rows:(h + 1) * rows], rb_ref, h, rb_ref[REL_BUCKETS - 1, h])
         for h in range(MOBA_HEADS)], axis=0)


def _sample_blocks_kernel(pt_ref, q_ref, rb_ref, *rest, past_len, page_size, n_pages, pages_per_step):
    pg = pages_per_step
    k_refs, v_refs = rest[:pg], rest[pg:2 * pg]
    acc_ref, m_ref, l_ref, g_ref, qa_ref, qf_ref, hm_ref, s_ref = rest[2 * pg:]
    j = pl.program_id(1)
    rows = SAMPLE_ATTN_ROWS
    ra = MOBA_HEADS * rows
    cols = page_size * MOBA_HEADS
    row_bits, head_bits = int(math.log2(rows)), int(math.log2(MOBA_HEADS))
    ppb = MOBA_BLOCK // page_size
    bps = pg // ppb

    @pl.when(j == 0)
    def _():
        qf = _sample_rows(q_ref, rows) * (MOBA_DH ** -0.5)
        qf_ref[...] = qf
        qa_ref[...] = qf.astype(BF16)
        r = lax.broadcasted_iota(jnp.int32, (ra, cols), 0)
        c = lax.broadcasted_iota(jnp.int32, (ra, cols), 1)
        hm_ref[...] = jnp.where(lax.shift_right_logical(r, row_bits) == (c & (MOBA_HEADS - 1)), 0.0, MASKED)
        m_ref[...] = jnp.zeros(m_ref.shape, F32)
        l_ref[...] = jnp.zeros(l_ref.shape, F32)
        g_ref[...] = jnp.zeros(g_ref.shape, F32)

    qa = qa_ref[...]
    qf = qf_ref[...]
    lane = lax.broadcasted_iota(jnp.int32, (ra, LANES), 1)
    m_all, l_all, g_all = m_ref[0], l_ref[0], g_ref[0]
    n_near = -(-(REL_MAX_DIST - 1) // page_size)
    assert n_near <= ppb
    for bb in range(bps):
        m_wide = jnp.full((ra, LANES), -jnp.inf, F32)
        ksum = jnp.zeros((MOBA_HEADS, MOBA_DH), F32)
        for p in range(ppb):
            r = bb * ppb + p
            k = k_refs[r][0]
            ksum = ksum + jnp.sum(k.reshape(page_size, MOBA_HEADS, MOBA_DH), axis=0)
            s = lax.dot_general(qa, k.astype(BF16), NT_DIMS, preferred_element_type=F32)
            s_ref[p] = s + hm_ref[...]
            if r >= pg - n_near:
                @pl.when(j == pl.num_programs(1) - 1)
                def _(r=r, p=p):
                    rr = lax.broadcasted_iota(jnp.int32, (ra, cols), 0)
                    cc = lax.broadcasted_iota(jnp.int32, (ra, cols), 1)
                    kpos = (n_pages - pg + r) * page_size + lax.shift_right_logical(cc, head_bits)
                    s_ref[p] = s_ref[p] + _sample_shifted_bias(past_len + (rr & (rows - 1)) - kpos, rb_ref, rows)
            s = s_ref[p]
            for c0 in range(0, cols, LANES):
                m_wide = jnp.maximum(m_wide, s[:, c0:c0 + LANES])
        m_b = jnp.max(m_wide, axis=1, keepdims=True)
        mb = jnp.broadcast_to(m_b, (ra, cols))
        l_wide = jnp.zeros((ra, LANES), F32)
        acc = jnp.zeros((ra, MOBA_DH), F32)
        for p in range(ppb):
            e = jnp.exp(s_ref[p] - mb)
            for c0 in range(0, cols, LANES):
                l_wide = l_wide + e[:, c0:c0 + LANES]
            acc = acc + jnp.dot(e.astype(BF16), v_refs[bb * ppb + p][0].astype(BF16), preferred_element_type=F32)
        acc_ref[0, bb] = acc
        kmean = ksum * (1.0 / MOBA_BLOCK)
        kmean_rows = jnp.concatenate(
            [jnp.broadcast_to(kmean[h:h + 1, :], (rows, MOBA_DH)) for h in range(MOBA_HEADS)], axis=0)
        gate = jnp.sum(qf * kmean_rows, axis=1, keepdims=True)
        here = lane == j * bps + bb
        m_all = jnp.where(here, m_b, m_all)
        l_all = jnp.where(here, jnp.sum(l_wide, axis=1, keepdims=True), l_all)
        g_all = jnp.where(here, gate, g_all)
    m_ref[0], l_ref[0], g_ref[0] = m_all, l_all, g_all


def sample_blocks(page_table, z3, cache_k, cache_v, layer, rel_bias, past_len):
    Ns, R, _ = z3.shape
    n_pages = page_table.shape[1]
    page_size = cache_k.shape[2] // MOBA_HEADS
    cols = page_size * MOBA_HEADS
    ra = MOBA_HEADS * SAMPLE_ATTN_ROWS
    pg = SAMPLE_PAGES_PER_STEP
    ppb = MOBA_BLOCK // page_size
    nb = n_pages // ppb
    assert n_pages % pg == 0 and pg % ppb == 0 and past_len == n_pages * page_size and nb <= LANES
    kern = functools.partial(_sample_blocks_kernel, past_len=past_len, page_size=page_size, n_pages=n_pages,
                             pages_per_step=pg)

    def page_spec(r):
        return pl.BlockSpec((None, 1, cols, MOBA_DH),
                            lambda n, j, pt: (layer, pt[n * n_pages + j * pg + r], 0, 0))

    stat_spec = pl.BlockSpec((1, ra, LANES), lambda n, j, pt: (n, 0, 0))
    stat_shape = jax.ShapeDtypeStruct((Ns, ra, LANES), F32)
    grid_spec = pltpu.PrefetchScalarGridSpec(
        num_scalar_prefetch=1,
        grid=(Ns, n_pages // pg),
        in_specs=([pl.BlockSpec((1, R, MOBA_W), lambda n, j, pt: (n, 0, COL_QB // MOBA_W)),
                   pl.BlockSpec(memory_space=pltpu.SMEM)]
                  + [page_spec(r) for r in range(pg)] + [page_spec(r) for r in range(pg)]),
        out_specs=[pl.BlockSpec((1, pg // ppb, ra, MOBA_DH), lambda n, j, pt: (n, j, 0, 0)),
                   stat_spec, stat_spec, stat_spec],
        scratch_shapes=[pltpu.VMEM((ra, MOBA_DH), BF16), pltpu.VMEM((ra, MOBA_DH), F32),
                        pltpu.VMEM((ra, cols), F32), pltpu.VMEM((ppb, ra, cols), F32)],
    )
    return pl.pallas_call(
        kern,
        grid_spec=grid_spec,
        out_shape=[jax.ShapeDtypeStruct((Ns, nb, ra, MOBA_DH), F32), stat_shape, stat_shape, stat_shape],
        compiler_params=_params("arbitrary", "arbitrary"),
        name="sample_blocks",
    )(page_table.reshape(-1), z3, rel_bias, *([cache_k] * pg), *([cache_v] * pg))


def _sample_combine_kernel(q_ref, kn_ref, vn_ref, acc_ref, m_ref, l_ref, g_ref, rb_ref, o_ref,
                           *, t_true, past_len, nb_past):
    rows = SAMPLE_ATTN_ROWS
    ra = MOBA_HEADS * rows
    row_bits = int(math.log2(rows))
    qa = (_sample_rows(q_ref, rows) * (MOBA_DH ** -0.5)).astype(BF16)
    kn = _sample_rows(kn_ref, rows).astype(BF16)
    vn = _sample_rows(vn_ref, rows).astype(BF16)
    s = lax.dot_general(qa, kn, NT_DIMS, preferred_element_type=F32)
    r = lax.broadcasted_iota(jnp.int32, (ra, ra), 0)
    c = lax.broadcasted_iota(jnp.int32, (ra, ra), 1)
    t, kt = r & (rows - 1), c & (rows - 1)
    valid = ((lax.shift_right_logical(r, row_bits) == lax.shift_right_logical(c, row_bits))
             & (kt <= t) & (kt < t_true))
    s = jnp.where(valid, s + _sample_shifted_bias(t - kt, rb_ref, rows), MASKED)
    m_own = jnp.max(s, axis=1, keepdims=True)
    e = jnp.exp(s - m_own)
    l_own = jnp.sum(e, axis=1, keepdims=True)
    acc_own = jnp.dot(e.astype(BF16), vn, preferred_element_type=F32)

    lane = lax.broadcasted_iota(jnp.int32, (ra, LANES), 1)
    pos = past_len + (lax.broadcasted_iota(jnp.int32, (ra, LANES), 0) & (rows - 1))
    own = lax.shift_right_logical(pos, int(math.log2(MOBA_BLOCK)))
    picks = _top_blocks(g_ref[0], lane, (lane < own) & (lane < nb_past))
    chosen = lane == picks[0]
    for idx in picks[1:]:
        chosen = chosen | (lane == idx)
    m_blk = jnp.where(chosen, m_ref[0], -jnp.inf)
    m_tot = jnp.maximum(m_own, jnp.max(m_blk, axis=1, keepdims=True))
    w_blk = jnp.exp(m_blk - m_tot)
    w_own = jnp.exp(m_own - m_tot)
    l_tot = w_own * l_own + jnp.sum(w_blk * l_ref[0], axis=1, keepdims=True)
    acc = w_own * acc_own
    for b in range(nb_past):
        acc = acc + w_blk[:, b:b + 1] * acc_ref[0, b]
    out = acc / l_tot
    o_ref[...] = jnp.zeros(o_ref.shape, F32)
    for h in range(MOBA_HEADS):
        o_ref[0, :rows, h * MOBA_DH:(h + 1) * MOBA_DH] = out[h * rows:(h + 1) * rows]


def sample_combine(z3, acc, m, l, g, rel_bias, t_true, past_len):
    Ns, R, _ = z3.shape
    nb_past, ra = acc.shape[1], acc.shape[2]
    assert t_true <= SAMPLE_ATTN_ROWS <= R
    kern = functools.partial(_sample_combine_kernel, t_true=t_true, past_len=past_len, nb_past=nb_past)

    def z_spec(col):
        return pl.BlockSpec((1, R, MOBA_W), lambda n: (n, 0, col // MOBA_W))

    stat_spec = pl.BlockSpec((1, ra, LANES), lambda n: (n, 0, 0))
    return pl.pallas_call(
        kern,
        grid=(Ns,),
        in_specs=[z_spec(COL_QB), z_spec(COL_KB), z_spec(COL_VB),
                  pl.BlockSpec((1, nb_past, ra, MOBA_DH), lambda n: (n, 0, 0, 0)),
                  stat_spec, stat_spec, stat_spec,
                  pl.BlockSpec(memory_space=pltpu.SMEM)],
        out_specs=pl.BlockSpec((1, R, MOBA_W), lambda n: (n, 0, 0)),
        out_shape=jax.ShapeDtypeStruct((Ns, R, MOBA_W), F32),
        compiler_params=_params("arbitrary"),
        name="sample_combine",
    )(z3, z3, z3, acc, m, l, g, rel_bias)


def _layer_tail(x, z, y_a, moba_o, w, layer):
    m = merge_gates(y_a, moba_o, z, w["w_pa"], w["w_pb"], layer)
    x = out_proj(m, w["w_o"], x, layer)
    return ffn(x, w["norm2_w"], w["w_ff1"], w["w_ff2"], layer)


def kernel(x_prompt, x_sample, cache_k, cache_v, state_ret, page_table, rel_bias, norm1_w, w_in, ret_norm_w,
           w_pa, w_pb, w_o, norm2_w, w_ff1, w_ff2, final_norm_w):
    depth = w_in.shape[0]
    B, S, D = x_prompt.shape
    Ns, Ts, _ = x_sample.shape
    n_phys, page_size = cache_k.shape[1], cache_k.shape[2]
    n_pages = page_table.shape[1]
    past_len = n_pages * page_size
    assert D == D_MODEL and S % MOBA_BLOCK == 0 and past_len % MOBA_BLOCK == 0 and Ts <= SAMPLE_ROWS

    w = dict(w_pa=w_pa.astype(BF16), w_pb=w_pb.astype(BF16), w_o=w_o.astype(BF16), norm2_w=norm2_w,
             w_ff1=w_ff1.astype(BF16), w_ff2=w_ff2.astype(BF16))
    w_in = w_in.astype(BF16)

    ret_chunk = _tile(S, 256)
    cos_p, sin_p = _rope_tables(np.arange(S))
    bias_tables = moba_bias_tables(rel_bias)
    zero_state = jnp.zeros((1, B, RET_HEADS, RET_DK, RET_DV), F32)
    xp = x_prompt.reshape(B * S, D)
    pk, pv, ps = [], [], []
    for l in range(depth):
        z = in_proj(xp, norm1_w, w_in, l)
        z3 = z.reshape(B, S, D_IN)
        y_a, s_fin = retention(z3, cos_p, sin_p, ret_norm_w, l, zero_state, 0, ret_chunk, ret_chunk)
        q_aug, k_aug, v_b = moba_select(z3)
        moba_o = moba_flash(q_aug, k_aug, v_b, bias_tables)
        xp = _layer_tail(xp, z, y_a.reshape(B * S, RET_V_W), moba_o.reshape(B * S, MOBA_W), w, l)
        pk.append(z3[:, :, COL_KB:COL_KB + MOBA_W].reshape(B, S, MOBA_HEADS, MOBA_DH))
        pv.append(z3[:, :, COL_VB:COL_VB + MOBA_W].reshape(B, S, MOBA_HEADS, MOBA_DH))
        ps.append(s_fin)
    y_prompt = final_norm(xp, final_norm_w).reshape(B, S, D)

    R = SAMPLE_ROWS
    cos_s, sin_s = _rope_tables(past_len + np.arange(R))
    ck = cache_k.reshape(depth, n_phys, page_size * MOBA_HEADS, MOBA_DH)
    cv = cache_v.reshape(depth, n_phys, page_size * MOBA_HEADS, MOBA_DH)
    xs = jnp.zeros((Ns, R, D), F32).at[:, :Ts].set(x_sample).reshape(Ns * R, D)
    sk, sv, ss = [], [], []
    for l in range(depth):
        z = in_proj(xs, norm1_w, w_in, l)
        z3 = z.reshape(Ns, R, D_IN)
        y_a, s_new = retention(z3, cos_s, sin_s, ret_norm_w, l, state_ret, l, R, Ts)
        blk_acc, blk_m, blk_l, blk_g = sample_blocks(page_table, z3, ck, cv, l, rel_bias, past_len)
        moba_o = sample_combine(z3, blk_acc, blk_m, blk_l, blk_g, rel_bias, Ts, past_len)
        xs = _layer_tail(xs, z, y_a.reshape(Ns * R, RET_V_W), moba_o.reshape(Ns * R, MOBA_W), w, l)
        sk.append(z3[:, :Ts, COL_KB:COL_KB + MOBA_W].reshape(Ns, Ts, MOBA_HEADS, MOBA_DH))
        sv.append(z3[:, :Ts, COL_VB:COL_VB + MOBA_W].reshape(Ns, Ts, MOBA_HEADS, MOBA_DH))
        ss.append(s_new)
    y_sample = final_norm(xs, final_norm_w).reshape(Ns, R, D)[:, :Ts]

    return (y_prompt, y_sample, jnp.stack(pk), jnp.stack(pv), jnp.stack(ps),
            jnp.stack(sk), jnp.stack(sv), jnp.stack(ss))
```

```python
import functools
import math

import numpy as np
import jax
import jax.numpy as jnp
from jax import lax
from jax.experimental import pallas as pl
from jax.experimental.pallas import tpu as pltpu

F32 = jnp.float32
BF16 = jnp.bfloat16

D_MODEL = 2048
RET_HEADS = 8
RET_DK = 128
RET_DV = 256
MOBA_HEADS = 8
MOBA_DH = 128
MOBA_BLOCK = 256
MOBA_TOPK = 3
REL_BUCKETS = 32
REL_MAX_DIST = 128
ROPE_BASE = 10000.0
NORM_EPS = 1e-6

RET_QK_W = RET_HEADS * RET_DK
RET_V_W = RET_HEADS * RET_DV
MOBA_W = MOBA_HEADS * MOBA_DH
COL_QA = 0
COL_KA = COL_QA + RET_QK_W
COL_VA = COL_KA + RET_QK_W
COL_GA = COL_VA + RET_V_W
COL_QB = COL_GA + RET_V_W
COL_KB = COL_QB + MOBA_W
COL_VB = COL_KB + MOBA_W
COL_GATE_A = COL_VB + MOBA_W
COL_GATE_B = COL_GATE_A + D_MODEL
D_IN = COL_GATE_B + D_MODEL

LANES = 128
SAMPLE_ROWS = 16
FLASH_STEPS = 1
SAMPLE_ATTN_ROWS = 8
SAMPLE_PAGES_PER_STEP = 16
MASKED = -1e9
LOG2_E = math.log2(math.e)
VMEM_LIMIT = 48 * 1024 * 1024

LOG_GAMMA = tuple(math.log(1.0 - 2.0 ** (-5.0 - h)) for h in range(RET_HEADS))

NT_DIMS = (((1,), (1,)), ((), ()))
TN_DIMS = (((0,), (0,)), ((), ()))


def _params(*sem):
    return pltpu.CompilerParams(dimension_semantics=sem, vmem_limit_bytes=VMEM_LIMIT)


def _tile(n, pref):
    t = min(n, pref)
    while n % t:
        t -= 1
    return t


def _rms(x, w):
    return x * lax.rsqrt(jnp.mean(x * x, axis=-1, keepdims=True) + NORM_EPS) * w


def _in_proj_kernel(x_ref, nw_ref, w_ref, o_ref, h_ref):
    @pl.when(pl.program_id(1) == 0)
    def _():
        h_ref[...] = _rms(x_ref[...], nw_ref[...]).astype(BF16)

    o_ref[...] = jnp.dot(h_ref[...], w_ref[...], preferred_element_type=F32)


def _layer_vec(w):
    return w.reshape(w.shape[0], 1, w.shape[1])


def in_proj(x, norm_w, w, layer):
    M, K = x.shape
    N = w.shape[2]
    tm, tn = _tile(M, 1024), _tile(N, 1024)
    return pl.pallas_call(
        _in_proj_kernel,
        grid=(M // tm, N // tn),
        in_specs=[pl.BlockSpec((tm, K), lambda i, j: (i, 0)),
                  pl.BlockSpec((None, 1, K), lambda i, j: (layer, 0, 0)),
                  pl.BlockSpec((None, K, tn), lambda i, j: (layer, 0, j))],
        out_specs=pl.BlockSpec((tm, tn), lambda i, j: (i, j)),
        out_shape=jax.ShapeDtypeStruct((M, N), F32),
        scratch_shapes=[pltpu.VMEM((tm, K), BF16)],
        compiler_params=_params("arbitrary", "arbitrary"),
        name="in_proj",
    )(x, _layer_vec(norm_w), w)


def _merge_kernel(ya_ref, mo_ref, ga_ref, gb_ref, wpa_ref, wpb_ref, o_ref):
    ua = jnp.dot(ya_ref[...].astype(BF16), wpa_ref[...], preferred_element_type=F32)
    ub = jnp.dot(mo_ref[...].astype(BF16), wpb_ref[...], preferred_element_type=F32)
    o_ref[...] = (jax.nn.sigmoid(ga_ref[...]) * ua + jax.nn.sigmoid(gb_ref[...]) * ub).astype(BF16)


def merge_gates(y_a, moba_o, z, w_pa, w_pb, layer):
    M = y_a.shape[0]
    tm, tn = _tile(M, 1024), 512
    ga0, gb0 = COL_GATE_A // tn, COL_GATE_B // tn
    return pl.pallas_call(
        _merge_kernel,
        grid=(M // tm, D_MODEL // tn),
        in_specs=[pl.BlockSpec((tm, RET_V_W), lambda i, j: (i, 0)),
                  pl.BlockSpec((tm, MOBA_W), lambda i, j: (i, 0)),
                  pl.BlockSpec((tm, tn), lambda i, j: (i, ga0 + j)),
                  pl.BlockSpec((tm, tn), lambda i, j: (i, gb0 + j)),
                  pl.BlockSpec((None, RET_V_W, tn), lambda i, j: (layer, 0, j)),
                  pl.BlockSpec((None, MOBA_W, tn), lambda i, j: (layer, 0, j))],
        out_specs=pl.BlockSpec((tm, tn), lambda i, j: (i, j)),
        out_shape=jax.ShapeDtypeStruct((M, D_MODEL), BF16),
        compiler_params=_params("arbitrary", "arbitrary"),
        name="merge_gates",
    )(y_a, moba_o, z, z, w_pa, w_pb)


def _out_proj_kernel(m_ref, w_ref, x_ref, o_ref):
    o_ref[...] = x_ref[...] + jnp.dot(m_ref[...], w_ref[...], preferred_element_type=F32)


def out_proj(m, w_o, x, layer):
    M, K = m.shape
    tm, tn = _tile(M, 1024), 1024
    return pl.pallas_call(
        _out_proj_kernel,
        grid=(M // tm, D_MODEL // tn),
        in_specs=[pl.BlockSpec((tm, K), lambda i, j: (i, 0)),
                  pl.BlockSpec((None, K, tn), lambda i, j: (layer, 0, j)),
                  pl.BlockSpec((tm, tn), lambda i, j: (i, j))],
        out_specs=pl.BlockSpec((tm, tn), lambda i, j: (i, j)),
        out_shape=jax.ShapeDtypeStruct((M, D_MODEL), F32),
        compiler_params=_params("arbitrary", "arbitrary"),
        name="out_proj",
    )(m, w_o, x)


def _ffn_kernel(x_ref, nw_ref, w1_ref, w2_ref, o_ref, h_ref):
    @pl.when(pl.program_id(1) == 0)
    def _():
        x = x_ref[...]
        h_ref[...] = _rms(x, nw_ref[...]).astype(BF16)
        o_ref[...] = x

    a = jnp.dot(h_ref[...], w1_ref[...], preferred_element_type=F32)
    a = jnp.square(jnp.maximum(a, 0.0)).astype(BF16)
    o_ref[...] += jnp.dot(a, w2_ref[...], preferred_element_type=F32)


def ffn(x, norm_w, w1, w2, layer):
    M, D = x.shape
    FF = w1.shape[2]
    tm, tf = _tile(M, 512), _tile(FF, 1024)
    return pl.pallas_call(
        _ffn_kernel,
        grid=(M // tm, FF // tf),
        in_specs=[pl.BlockSpec((tm, D), lambda i, f: (i, 0)),
                  pl.BlockSpec((None, 1, D), lambda i, f: (layer, 0, 0)),
                  pl.BlockSpec((None, D, tf), lambda i, f: (layer, 0, f)),
                  pl.BlockSpec((None, tf, D), lambda i, f: (layer, f, 0))],
        out_specs=pl.BlockSpec((tm, D), lambda i, f: (i, 0)),
        out_shape=jax.ShapeDtypeStruct((M, D), F32),
        scratch_shapes=[pltpu.VMEM((tm, D), BF16)],
        compiler_params=_params("arbitrary", "arbitrary"),
        name="ffn",
    )(x, _layer_vec(norm_w), w1, w2)


def _final_norm_kernel(x_ref, nw_ref, o_ref):
    o_ref[...] = _rms(x_ref[...], nw_ref[...])


def final_norm(x, norm_w):
    M, D = x.shape
    tm = _tile(M, 1024)
    return pl.pallas_call(
        _final_norm_kernel,
        grid=(M // tm,),
        in_specs=[pl.BlockSpec((tm, D), lambda i: (i, 0)),
                  pl.BlockSpec((1, D), lambda i: (0, 0))],
        out_specs=pl.BlockSpec((tm, D), lambda i: (i, 0)),
        out_shape=jax.ShapeDtypeStruct((M, D), F32),
        compiler_params=_params("arbitrary"),
        name="final_norm",
    )(x, norm_w.reshape(1, D))


def _rope_tables(pos):
    half = RET_DK // 2
    inv = ROPE_BASE ** (-np.arange(half, dtype=np.float64) / half)
    ang = np.asarray(pos, np.float64)[:, None] * inv[None, :]
    cos = np.concatenate([np.cos(ang), np.cos(ang)], axis=1)
    sin = np.concatenate([-np.sin(ang), np.sin(ang)], axis=1)
    return jnp.asarray(cos, F32), jnp.asarray(sin, F32)


def _retention_kernel(q_ref, k_ref, v_ref, g_ref, cos_ref, sin_ref, nw_ref, s0_ref, y_ref, s_ref, dm_ref,
                      *, c_rows, c_true):
    c = pl.program_id(1)

    @pl.when(c == 0)
    def _():
        s_ref[...] = s0_ref[...]

    @pl.when((pl.program_id(0) == 0) & (c == 0))
    def _():
        i = lax.broadcasted_iota(jnp.int32, (c_rows, c_rows), 0)
        j = lax.broadcasted_iota(jnp.int32, (c_rows, c_rows), 1)
        diff = (i - j).astype(F32)
        for h in range(RET_HEADS):
            dm_ref[h] = jnp.where(diff >= 0, jnp.exp(LOG_GAMMA[h] * jnp.maximum(diff, 0.0)), 0.0)

    cos = cos_ref[...]
    sin = sin_ref[...]
    row = lax.broadcasted_iota(jnp.int32, (c_rows, 1), 0).astype(F32)
    for h in range(RET_HEADS):
        lg = LOG_GAMMA[h]
        q = q_ref[0, :, h * RET_DK:(h + 1) * RET_DK]
        k = k_ref[0, :, h * RET_DK:(h + 1) * RET_DK]
        v = v_ref[0, :, h * RET_DV:(h + 1) * RET_DV].astype(BF16)
        q = q * cos + pltpu.roll(q, RET_DK // 2, 1) * sin
        k = (k * cos + pltpu.roll(k, RET_DK // 2, 1) * sin) * (RET_DK ** -0.5)
        qb = q.astype(BF16)
        scores = lax.dot_general(qb, k.astype(BF16), NT_DIMS, preferred_element_type=F32) * dm_ref[h]
        o = jnp.dot(scores.astype(BF16), v, preferred_element_type=F32)
        s = s_ref[0, h]
        q_dec = jnp.exp(lg * (row + 1.0))
        o = o + jnp.dot(qb, s.astype(BF16), preferred_element_type=F32) * q_dec
        k_dec = jnp.exp(lg * (c_true - 1.0 - row))
        kd = k * k_dec
        if c_true < c_rows:
            kd = jnp.where(row < c_true, kd, 0.0)
        s_ref[0, h] = math.exp(lg * c_true) * s + lax.dot_general(
            kd.astype(BF16), v, TN_DIMS, preferred_element_type=F32)
        y = _rms(o, nw_ref[:, h * RET_DV:(h + 1) * RET_DV])
        g = g_ref[0, :, h * RET_DV:(h + 1) * RET_DV]
        y_ref[0, :, h * RET_DV:(h + 1) * RET_DV] = (y * (g * jax.nn.sigmoid(g))).astype(BF16)


def retention(z3, cos, sin, ret_norm_w, layer, s0, s0_layer, c_rows, c_true):
    N, T, _ = z3.shape
    nc = T // c_rows
    kern = functools.partial(_retention_kernel, c_rows=c_rows, c_true=c_true)
    return pl.pallas_call(
        kern,
        grid=(N, nc),
        in_specs=[pl.BlockSpec((1, c_rows, RET_QK_W), lambda n, c: (n, c, COL_QA // RET_QK_W)),
                  pl.BlockSpec((1, c_rows, RET_QK_W), lambda n, c: (n, c, COL_KA // RET_QK_W)),
                  pl.BlockSpec((1, c_rows, RET_V_W), lambda n, c: (n, c, COL_VA // RET_V_W)),
                  pl.BlockSpec((1, c_rows, RET_V_W), lambda n, c: (n, c, COL_GA // RET_V_W)),
                  pl.BlockSpec((c_rows, RET_DK), lambda n, c: (c, 0)),
                  pl.BlockSpec((c_rows, RET_DK), lambda n, c: (c, 0)),
                  pl.BlockSpec((None, 1, RET_V_W), lambda n, c: (layer, 0, 0)),
                  pl.BlockSpec((None, 1, RET_HEADS, RET_DK, RET_DV), lambda n, c: (s0_layer, n, 0, 0, 0))],
        out_specs=[pl.BlockSpec((1, c_rows, RET_V_W), lambda n, c: (n, c, 0)),
                   pl.BlockSpec((1, RET_HEADS, RET_DK, RET_DV), lambda n, c: (n, 0, 0, 0))],
        out_shape=[jax.ShapeDtypeStruct((N, T, RET_V_W), BF16),
                   jax.ShapeDtypeStruct((N, RET_HEADS, RET_DK, RET_DV), F32)],
        scratch_shapes=[pltpu.VMEM((RET_HEADS, c_rows, c_rows), F32)],
        compiler_params=_params("arbitrary", "arbitrary"),
        name="retention",
    )(z3, z3, z3, z3, cos, sin, _layer_vec(ret_norm_w), s0)


def _t5_bucket_np(dist):
    n = np.maximum(dist, 0)
    max_exact = REL_BUCKETS // 2
    nf = np.maximum(n, 1).astype(np.float64)
    large = max_exact + (np.log(nf / max_exact) / math.log(REL_MAX_DIST / max_exact)
                         * (REL_BUCKETS - max_exact)).astype(np.int64)
    large = np.minimum(large, REL_BUCKETS - 1)
    return np.where(n < max_exact, n, large).astype(np.int32)


def _t5_bucket(dist):
    n = jnp.maximum(dist, 0)
    max_exact = REL_BUCKETS // 2
    nf = jnp.maximum(n, 1).astype(F32)
    large = max_exact + (jnp.log(nf / max_exact) / math.log(REL_MAX_DIST / max_exact)
                         * (REL_BUCKETS - max_exact)).astype(jnp.int32)
    large = jnp.minimum(large, REL_BUCKETS - 1)
    return jnp.where(n < max_exact, n, large)


def _bias_lookup(bucket, rb_ref, h, shift):
    out = jnp.zeros(bucket.shape, F32)
    for b in range(REL_BUCKETS):
        out = jnp.where(bucket == b, rb_ref[b, h] - shift, out)
    return out


def _top_blocks(gate, blk, eligible, axis=1):
    none = LANES
    g = jnp.where(eligible, gate, -jnp.inf)
    picks = []
    for _ in range(MOBA_TOPK):
        m = jnp.max(g, axis=axis, keepdims=True)
        cand = (g == m) & (m > -jnp.inf)
        idx = jnp.min(jnp.where(cand, blk, none), axis=axis, keepdims=True)
        idx = jnp.where(idx == none, -1, idx)
        picks.append(idx)
        g = jnp.where(blk == idx, -jnp.inf, g)
    return picks


def _moba_bias_kernel(rb_ref, bk_ref, o_ref):
    h = pl.program_id(0)
    far = rb_ref[REL_BUCKETS - 1, h]
    key = lax.broadcasted_iota(jnp.int32, (MOBA_BLOCK, MOBA_BLOCK), 0)
    query = lax.broadcasted_iota(jnp.int32, (MOBA_BLOCK, MOBA_BLOCK), 1)
    o_ref[0, 0] = jnp.where(query >= key, _bias_lookup(bk_ref[0], rb_ref, h, far) * LOG2_E, MASKED)
    o_ref[0, 1] = _bias_lookup(bk_ref[1], rb_ref, h, far) * LOG2_E


def moba_bias_tables(rel_bias):
    d = np.arange(MOBA_BLOCK)[None, :] - np.arange(MOBA_BLOCK)[:, None]
    buckets = jnp.asarray(np.stack([_t5_bucket_np(d), _t5_bucket_np(d + MOBA_BLOCK)]))
    return pl.pallas_call(
        _moba_bias_kernel,
        grid=(MOBA_HEADS,),
        in_specs=[pl.BlockSpec(memory_space=pltpu.SMEM),
                  pl.BlockSpec((2, MOBA_BLOCK, MOBA_BLOCK), lambda h: (0, 0, 0))],
        out_specs=pl.BlockSpec((1, 2, MOBA_BLOCK, MOBA_BLOCK), lambda h: (h, 0, 0, 0)),
        out_shape=jax.ShapeDtypeStruct((MOBA_HEADS, 2, MOBA_BLOCK, MOBA_BLOCK), F32),
        compiler_params=_params("arbitrary"),
        name="moba_bias_tables",
    )(rel_bias, buckets)


def _moba_select_kernel(q_ref, k_ref, v_ref, oh_ref, qa_ref, ka_ref, vb_ref, *, nb):
    q = q_ref[0]
    k = k_ref[0]
    S = q.shape[0]
    nbp = -(-nb // 8) * 8
    assert nbp <= LANES
    kmean = jnp.mean(k.reshape(nb, MOBA_BLOCK, MOBA_DH), axis=1)
    if nb < nbp:
        kmean = jnp.concatenate([kmean, jnp.zeros((nbp - nb, MOBA_DH), F32)], axis=0)
    gate = lax.dot_general(kmean, q, NT_DIMS, precision=lax.Precision.HIGHEST, preferred_element_type=F32)
    blk = lax.broadcasted_iota(jnp.int32, (nbp, S), 0)
    own = lax.shift_right_logical(lax.broadcasted_iota(jnp.int32, (nbp, S), 1), int(math.log2(MOBA_BLOCK)))
    blk_f = blk.astype(F32)
    picks = _top_blocks(gate, blk_f, blk < own, axis=0)
    allowed = blk == own
    for idx in picks:
        allowed = allowed | (blk_f == idx)
    eye = jnp.where(lax.broadcasted_iota(jnp.int32, (nbp, LANES), 0)
                    == lax.broadcasted_iota(jnp.int32, (nbp, LANES), 1), 1.0, 0.0).astype(BF16)
    allowed_q = lax.dot_general(jnp.where(allowed, 1.0, 0.0).astype(BF16), eye, TN_DIMS,
                                preferred_element_type=F32)
    qa_ref[0, 0, :, :MOBA_DH] = (q * (MOBA_DH ** -0.5 * LOG2_E)).astype(BF16)
    qa_ref[0, 0, :, MOBA_DH:] = ((1.0 - allowed_q) * MASKED).astype(BF16)
    ka_ref[0, 0, :, :MOBA_DH] = k.astype(BF16)
    ka_ref[0, 0, :, MOBA_DH:] = oh_ref[...]
    vb_ref[0, 0, :, :MOBA_DH] = v_ref[0].astype(BF16)
    vb_ref[0, 0, :, MOBA_DH:] = jnp.where(lax.broadcasted_iota(jnp.int32, (S, MOBA_DH), 1) == 0, 1.0, 0.0).astype(BF16)


def moba_select(z3):
    B, S, _ = z3.shape
    nb = S // MOBA_BLOCK
    kern = functools.partial(_moba_select_kernel, nb=nb)
    onehot = jnp.asarray(np.arange(S)[:, None] // MOBA_BLOCK == np.arange(LANES)[None, :], BF16)
    return pl.pallas_call(
        kern,
        grid=(B, MOBA_HEADS),
        in_specs=[pl.BlockSpec((1, S, MOBA_DH), lambda b, h: (b, 0, COL_QB // MOBA_DH + h)),
                  pl.BlockSpec((1, S, MOBA_DH), lambda b, h: (b, 0, COL_KB // MOBA_DH + h)),
                  pl.BlockSpec((1, S, MOBA_DH), lambda b, h: (b, 0, COL_VB // MOBA_DH + h)),
                  pl.BlockSpec((S, LANES), lambda b, h: (0, 0))],
        out_specs=[pl.BlockSpec((1, 1, S, 2 * MOBA_DH), lambda b, h: (b, h, 0, 0)),
                   pl.BlockSpec((1, 1, S, 2 * MOBA_DH), lambda b, h: (b, h, 0, 0)),
                   pl.BlockSpec((1, 1, S, 2 * MOBA_DH), lambda b, h: (b, h, 0, 0))],
        out_shape=[jax.ShapeDtypeStruct((B, MOBA_HEADS, S, 2 * MOBA_DH), BF16)] * 3,
        compiler_params=_params("arbitrary", "arbitrary"),
        name="moba_select",
    )(z3, z3, z3, onehot)


def _moba_flash_tile(qi, q_ref, k_ref, v_ref, bias_ref, o_ref, s_ref, p_ref):
    q = q_ref[0, 0, qi * MOBA_BLOCK:(qi + 1) * MOBA_BLOCK, :]
    n = (qi + 1) * MOBA_BLOCK
    s_ref[:n, :] = lax.dot_general(k_ref[0, 0, :n, :], q, NT_DIMS, preferred_element_type=F32)
    m8 = jnp.full((8, MOBA_BLOCK), -jnp.inf, F32)
    for j in range(qi + 1):
        rows = slice(j * MOBA_BLOCK, (j + 1) * MOBA_BLOCK)
        s = s_ref[rows, :]
        if j >= qi - 1:
            s = s + bias_ref[0, qi - j]
            s_ref[rows, :] = s
        m8 = jnp.maximum(m8, jnp.max(s.reshape(MOBA_BLOCK // 8, 8, MOBA_BLOCK), axis=0))
    m = jnp.max(m8, axis=0, keepdims=True)
    for j in range(qi + 1):
        rows = slice(j * MOBA_BLOCK, (j + 1) * MOBA_BLOCK)
        p_ref[rows, :] = jnp.exp2(s_ref[rows, :] - m).astype(BF16)
    acc = lax.dot_general(p_ref[:n, :], v_ref[0, 0, :n, :], TN_DIMS, preferred_element_type=F32)
    o_ref[0, qi * MOBA_BLOCK:(qi + 1) * MOBA_BLOCK, :] = (
        acc[:, :MOBA_DH] / acc[:, MOBA_DH:MOBA_DH + 1]).astype(BF16)


def _flash_groups(nb, steps):
    pairs = [(i, nb - 1 - i) for i in range(nb // 2)]
    return [sum(pairs[g::steps], ()) for g in range(steps)]


def _moba_flash_kernel(q_ref, k_ref, v_ref, bias_ref, o_ref, s_ref, p_ref, *, groups):
    i = pl.program_id(2)
    for g, blocks in enumerate(groups):
        @pl.when(i == g)
        def _(blocks=blocks):
            for qi in blocks:
                _moba_flash_tile(qi, q_ref, k_ref, v_ref, bias_ref, o_ref, s_ref, p_ref)


def moba_flash(q_aug, k_aug, v, bias):
    B, H, S, _ = q_aug.shape
    nb = S // MOBA_BLOCK
    steps = FLASH_STEPS if nb % (2 * FLASH_STEPS) == 0 else 1
    groups = _flash_groups(nb, steps) if nb % 2 == 0 else [tuple(range(nb))]
    return pl.pallas_call(
        functools.partial(_moba_flash_kernel, groups=groups),
        grid=(B, H, len(groups)),
        in_specs=[pl.BlockSpec((1, 1, S, 2 * MOBA_DH), lambda b, h, i: (b, h, 0, 0)),
                  pl.BlockSpec((1, 1, S, 2 * MOBA_DH), lambda b, h, i: (b, h, 0, 0)),
                  pl.BlockSpec((1, 1, S, 2 * MOBA_DH), lambda b, h, i: (b, h, 0, 0)),
                  pl.BlockSpec((1, 2, MOBA_BLOCK, MOBA_BLOCK), lambda b, h, i: (h, 0, 0, 0))],
        out_specs=pl.BlockSpec((1, S, MOBA_DH), lambda b, h, i: (b, 0, h)),
        out_shape=jax.ShapeDtypeStruct((B, S, MOBA_W), BF16),
        scratch_shapes=[pltpu.VMEM((S, MOBA_BLOCK), F32), pltpu.VMEM((S, MOBA_BLOCK), BF16)],
        compiler_params=_params("arbitrary", "arbitrary", "arbitrary"),
        name="moba_flash",
    )(q_aug, k_aug, v, bias)


def _sample_rows(ref, rows):
    return jnp.concatenate([ref[0, :rows, h * MOBA_DH:(h + 1) * MOBA_DH] for h in range(MOBA_HEADS)], axis=0)


def _sample_shifted_bias(dist, rb_ref, rows):
    bucket = _t5_bucket(dist)
    return jnp.concatenate(
        [_bias_lookup(bucket[h * rows:(h + 1) * rows], rb_ref, h, rb_ref[REL_BUCKETS - 1, h])
         for h in range(MOBA_HEADS)], axis=0)


def _sample_blocks_kernel(pt_ref, q_ref, rb_ref, *rest, past_len, page_size, n_pages, pages_per_step):
    pg = pages_per_step
    k_refs, v_refs = rest[:pg], rest[pg:2 * pg]
    acc_ref, m_ref, l_ref, g_ref, qa_ref, qf_ref, hm_ref, s_ref = rest[2 * pg:]
    j = pl.program_id(1)
    rows = SAMPLE_ATTN_ROWS
    ra = MOBA_HEADS * rows
    cols = page_size * MOBA_HEADS
    row_bits, head_bits = int(math.log2(rows)), int(math.log2(MOBA_HEADS))
    ppb = MOBA_BLOCK // page_size
    bps = pg // ppb

    @pl.when(j == 0)
    def _():
        qf = _sample_rows(q_ref, rows) * (MOBA_DH ** -0.5)
        qf_ref[...] = qf
        qa_ref[...] = qf.astype(BF16)
        r = lax.broadcasted_iota(jnp.int32, (ra, cols), 0)
        c = lax.broadcasted_iota(jnp.int32, (ra, cols), 1)
        hm_ref[...] = jnp.where(lax.shift_right_logical(r, row_bits) == (c & (MOBA_HEADS - 1)), 0.0, MASKED)
        m_ref[...] = jnp.zeros(m_ref.shape, F32)
        l_ref[...] = jnp.zeros(l_ref.shape, F32)
        g_ref[...] = jnp.zeros(g_ref.shape, F32)

    qa = qa_ref[...]
    qf = qf_ref[...]
    lane = lax.broadcasted_iota(jnp.int32, (ra, LANES), 1)
    m_all, l_all, g_all = m_ref[0], l_ref[0], g_ref[0]
    n_near = -(-(REL_MAX_DIST - 1) // page_size)
    assert n_near <= ppb
    for bb in range(bps):
        m_wide = jnp.full((ra, LANES), -jnp.inf, F32)
        ksum = jnp.zeros((MOBA_HEADS, MOBA_DH), F32)
        for p in range(ppb):
            r = bb * ppb + p
            k = k_refs[r][0]
            ksum = ksum + jnp.sum(k.reshape(page_size, MOBA_HEADS, MOBA_DH), axis=0)
            s = lax.dot_general(qa, k.astype(BF16), NT_DIMS, preferred_element_type=F32)
            s_ref[p] = s + hm_ref[...]
            if r >= pg - n_near:
                @pl.when(j == pl.num_programs(1) - 1)
                def _(r=r, p=p):
                    rr = lax.broadcasted_iota(jnp.int32, (ra, cols), 0)
                    cc = lax.broadcasted_iota(jnp.int32, (ra, cols), 1)
                    kpos = (n_pages - pg + r) * page_size + lax.shift_right_logical(cc, head_bits)
                    s_ref[p] = s_ref[p] + _sample_shifted_bias(past_len + (rr & (rows - 1)) - kpos, rb_ref, rows)
            s = s_ref[p]
            for c0 in range(0, cols, LANES):
                m_wide = jnp.maximum(m_wide, s[:, c0:c0 + LANES])
        m_b = jnp.max(m_wide, axis=1, keepdims=True)
        mb = jnp.broadcast_to(m_b, (ra, cols))
        l_wide = jnp.zeros((ra, LANES), F32)
        acc = jnp.zeros((ra, MOBA_DH), F32)
        for p in range(ppb):
            e = jnp.exp(s_ref[p] - mb)
            for c0 in range(0, cols, LANES):
                l_wide = l_wide + e[:, c0:c0 + LANES]
            acc = acc + jnp.dot(e.astype(BF16), v_refs[bb * ppb + p][0].astype(BF16), preferred_element_type=F32)
        acc_ref[0, bb] = acc
        kmean = ksum * (1.0 / MOBA_BLOCK)
        kmean_rows = jnp.concatenate(
            [jnp.broadcast_to(kmean[h:h + 1, :], (rows, MOBA_DH)) for h in range(MOBA_HEADS)], axis=0)
        gate = jnp.sum(qf * kmean_rows, axis=1, keepdims=True)
        here = lane == j * bps + bb
        m_all = jnp.where(here, m_b, m_all)
        l_all = jnp.where(here, jnp.sum(l_wide, axis=1, keepdims=True), l_all)
        g_all = jnp.where(here, gate, g_all)
    m_ref[0], l_ref[0], g_ref[0] = m_all, l_all, g_all


def sample_blocks(page_table, z3, cache_k, cache_v, layer, rel_bias, past_len):
    Ns, R, _ = z3.shape
    n_pages = page_table.shape[1]
    page_size = cache_k.shape[2] // MOBA_HEADS
    cols = page_size * MOBA_HEADS
    ra = MOBA_HEADS * SAMPLE_ATTN_ROWS
    pg = SAMPLE_PAGES_PER_STEP
    ppb = MOBA_BLOCK // page_size
    nb = n_pages // ppb
    assert n_pages % pg == 0 and pg % ppb == 0 and past_len == n_pages * page_size and nb <= LANES
    kern = functools.partial(_sample_blocks_kernel, past_len=past_len, page_size=page_size, n_pages=n_pages,
                             pages_per_step=pg)

    def page_spec(r):
        return pl.BlockSpec((None, 1, cols, MOBA_DH),
                            lambda n, j, pt: (layer, pt[n * n_pages + j * pg + r], 0, 0))

    stat_spec = pl.BlockSpec((1, ra, LANES), lambda n, j, pt: (n, 0, 0))
    stat_shape = jax.ShapeDtypeStruct((Ns, ra, LANES), F32)
    grid_spec = pltpu.PrefetchScalarGridSpec(
        num_scalar_prefetch=1,
        grid=(Ns, n_pages // pg),
        in_specs=([pl.BlockSpec((1, R, MOBA_W), lambda n, j, pt: (n, 0, COL_QB // MOBA_W)),
                   pl.BlockSpec(memory_space=pltpu.SMEM)]
                  + [page_spec(r) for r in range(pg)] + [page_spec(r) for r in range(pg)]),
        out_specs=[pl.BlockSpec((1, pg // ppb, ra, MOBA_DH), lambda n, j, pt: (n, j, 0, 0)),
                   stat_spec, stat_spec, stat_spec],
        scratch_shapes=[pltpu.VMEM((ra, MOBA_DH), BF16), pltpu.VMEM((ra, MOBA_DH), F32),
                        pltpu.VMEM((ra, cols), F32), pltpu.VMEM((ppb, ra, cols), F32)],
    )
    return pl.pallas_call(
        kern,
        grid_spec=grid_spec,
        out_shape=[jax.ShapeDtypeStruct((Ns, nb, ra, MOBA_DH), F32), stat_shape, stat_shape, stat_shape],
        compiler_params=_params("arbitrary", "arbitrary"),
        name="sample_blocks",
    )(page_table.reshape(-1), z3, rel_bias, *([cache_k] * pg), *([cache_v] * pg))


def _sample_combine_kernel(q_ref, kn_ref, vn_ref, acc_ref, m_ref, l_ref, g_ref, rb_ref, o_ref,
                           *, t_true, past_len, nb_past):
    rows = SAMPLE_ATTN_ROWS
    ra = MOBA_HEADS * rows
    row_bits = int(math.log2(rows))
    qa = (_sample_rows(q_ref, rows) * (MOBA_DH ** -0.5)).astype(BF16)
    kn = _sample_rows(kn_ref, rows).astype(BF16)
    vn = _sample_rows(vn_ref, rows).astype(BF16)
    s = lax.dot_general(qa, kn, NT_DIMS, preferred_element_type=F32)
    r = lax.broadcasted_iota(jnp.int32, (ra, ra), 0)
    c = lax.broadcasted_iota(jnp.int32, (ra, ra), 1)
    t, kt = r & (rows - 1), c & (rows - 1)
    valid = ((lax.shift_right_logical(r, row_bits) == lax.shift_right_logical(c, row_bits))
             & (kt <= t) & (kt < t_true))
    s = jnp.where(valid, s + _sample_shifted_bias(t - kt, rb_ref, rows), MASKED)
    m_own = jnp.max(s, axis=1, keepdims=True)
    e = jnp.exp(s - m_own)
    l_own = jnp.sum(e, axis=1, keepdims=True)
    acc_own = jnp.dot(e.astype(BF16), vn, preferred_element_type=F32)

    lane = lax.broadcasted_iota(jnp.int32, (ra, LANES), 1)
    pos = past_len + (lax.broadcasted_iota(jnp.int32, (ra, LANES), 0) & (rows - 1))
    own = lax.shift_right_logical(pos, int(math.log2(MOBA_BLOCK)))
    picks = _top_blocks(g_ref[0], lane, (lane < own) & (lane < nb_past))
    chosen = lane == picks[0]
    for idx in picks[1:]:
        chosen = chosen | (lane == idx)
    m_blk = jnp.where(chosen, m_ref[0], -jnp.inf)
    m_tot = jnp.maximum(m_own, jnp.max(m_blk, axis=1, keepdims=True))
    w_blk = jnp.exp(m_blk - m_tot)
    w_own = jnp.exp(m_own - m_tot)
    l_tot = w_own * l_own + jnp.sum(w_blk * l_ref[0], axis=1, keepdims=True)
    acc = w_own * acc_own
    for b in range(nb_past):
        acc = acc + w_blk[:, b:b + 1] * acc_ref[0, b]
    out = acc / l_tot
    o_ref[...] = jnp.zeros(o_ref.shape, F32)
    for h in range(MOBA_HEADS):
        o_ref[0, :rows, h * MOBA_DH:(h + 1) * MOBA_DH] = out[h * rows:(h + 1) * rows]


def sample_combine(z3, acc, m, l, g, rel_bias, t_true, past_len):
    Ns, R, _ = z3.shape
    nb_past, ra = acc.shape[1], acc.shape[2]
    assert t_true <= SAMPLE_ATTN_ROWS <= R
    kern = functools.partial(_sample_combine_kernel, t_true=t_true, past_len=past_len, nb_past=nb_past)

    def z_spec(col):
        return pl.BlockSpec((1, R, MOBA_W), lambda n: (n, 0, col // MOBA_W))

    stat_spec = pl.BlockSpec((1, ra, LANES), lambda n: (n, 0, 0))
    return pl.pallas_call(
        kern,
        grid=(Ns,),
        in_specs=[z_spec(COL_QB), z_spec(COL_KB), z_spec(COL_VB),
                  pl.BlockSpec((1, nb_past, ra, MOBA_DH), lambda n: (n, 0, 0, 0)),
                  stat_spec, stat_spec, stat_spec,
                  pl.BlockSpec(memory_space=pltpu.SMEM)],
        out_specs=pl.BlockSpec((1, R, MOBA_W), lambda n: (n, 0, 0)),
        out_shape=jax.ShapeDtypeStruct((Ns, R, MOBA_W), F32),
        compiler_params=_params("arbitrary"),
        name="sample_combine",
    )(z3, z3, z3, acc, m, l, g, rel_bias)


def _layer_tail(x, z, y_a, moba_o, w, layer):
    m = merge_gates(y_a, moba_o, z, w["w_pa"], w["w_pb"], layer)
    x = out_proj(m, w["w_o"], x, layer)
    return ffn(x, w["norm2_w"], w["w_ff1"], w["w_ff2"], layer)


def kernel(x_prompt, x_sample, cache_k, cache_v, state_ret, page_table, rel_bias, norm1_w, w_in, ret_norm_w,
           w_pa, w_pb, w_o, norm2_w, w_ff1, w_ff2, final_norm_w):
    depth = w_in.shape[0]
    B, S, D = x_prompt.shape
    Ns, Ts, _ = x_sample.shape
    n_phys, page_size = cache_k.shape[1], cache_k.shape[2]
    n_pages = page_table.shape[1]
    past_len = n_pages * page_size
    assert D == D_MODEL and S % MOBA_BLOCK == 0 and past_len % MOBA_BLOCK == 0 and Ts <= SAMPLE_ROWS

    w = dict(w_pa=w_pa.astype(BF16), w_pb=w_pb.astype(BF16), w_o=w_o.astype(BF16), norm2_w=norm2_w,
             w_ff1=w_ff1.astype(BF16), w_ff2=w_ff2.astype(BF16))
    w_in = w_in.astype(BF16)

    ret_chunk = _tile(S, 256)
    cos_p, sin_p = _rope_tables(np.arange(S))
    bias_tables = moba_bias_tables(rel_bias)
    zero_state = jnp.zeros((1, B, RET_HEADS, RET_DK, RET_DV), F32)
    xp = x_prompt.reshape(B * S, D)
    pk, pv, ps = [], [], []
    for l in range(depth):
        z = in_proj(xp, norm1_w, w_in, l)
        z3 = z.reshape(B, S, D_IN)
        y_a, s_fin = retention(z3, cos_p, sin_p, ret_norm_w, l, zero_state, 0, ret_chunk, ret_chunk)
        q_aug, k_aug, v_b = moba_select(z3)
        moba_o = moba_flash(q_aug, k_aug, v_b, bias_tables)
        xp = _layer_tail(xp, z, y_a.reshape(B * S, RET_V_W), moba_o.reshape(B * S, MOBA_W), w, l)
        pk.append(z3[:, :, COL_KB:COL_KB + MOBA_W].reshape(B, S, MOBA_HEADS, MOBA_DH))
        pv.append(z3[:, :, COL_VB:COL_VB + MOBA_W].reshape(B, S, MOBA_HEADS, MOBA_DH))
        ps.append(s_fin)
    y_prompt = final_norm(xp, final_norm_w).reshape(B, S, D)

    R = SAMPLE_ROWS
    cos_s, sin_s = _rope_tables(past_len + np.arange(R))
    ck = cache_k.reshape(depth, n_phys, page_size * MOBA_HEADS, MOBA_DH)
    cv = cache_v.reshape(depth, n_phys, page_size * MOBA_HEADS, MOBA_DH)
    xs = jnp.zeros((Ns, R, D), F32).at[:, :Ts].set(x_sample).reshape(Ns * R, D)
    sk, sv, ss = [], [], []
    for l in range(depth):
        z = in_proj(xs, norm1_w, w_in, l)
        z3 = z.reshape(Ns, R, D_IN)
        y_a, s_new = retention(z3, cos_s, sin_s, ret_norm_w, l, state_ret, l, R, Ts)
        blk_acc, blk_m, blk_l, blk_g = sample_blocks(page_table, z3, ck, cv, l, rel_bias, past_len)
        moba_o = sample_combine(z3, blk_acc, blk_m, blk_l, blk_g, rel_bias, Ts, past_len)
        xs = _layer_tail(xs, z, y_a.reshape(Ns * R, RET_V_W), moba_o.reshape(Ns * R, MOBA_W), w, l)
        sk.append(z3[:, :Ts, COL_KB:COL_KB + MOBA_W].reshape(Ns, Ts, MOBA_HEADS, MOBA_DH))
        sv.append(z3[:, :Ts, COL_VB:COL_VB + MOBA_W].reshape(Ns, Ts, MOBA_HEADS, MOBA_DH))
        ss.append(s_new)
    y_sample = final_norm(xs, final_norm_w).reshape(Ns, R, D)[:, :Ts]

    return (y_prompt, y_sample, jnp.stack(pk), jnp.stack(pv), jnp.stack(ps),
            jnp.stack(sk), jnp.stack(sv), jnp.stack(ss))
```

```python
import functools
import math

import numpy as np
import jax
import jax.numpy as jnp
from jax import lax
from jax.experimental import pallas as pl
from jax.experimental.pallas import tpu as pltpu

F32 = jnp.float32
BF16 = jnp.bfloat16

D_MODEL = 2048
RET_HEADS = 8
RET_DK = 128
RET_DV = 256
MOBA_HEADS = 8
MOBA_DH = 128
MOBA_BLOCK = 256
MOBA_TOPK = 3
REL_BUCKETS = 32
REL_MAX_DIST = 128
ROPE_BASE = 10000.0
NORM_EPS = 1e-6

RET_QK_W = RET_HEADS * RET_DK
RET_V_W = RET_HEADS * RET_DV
MOBA_W = MOBA_HEADS * MOBA_DH
COL_QA = 0
COL_KA = COL_QA + RET_QK_W
COL_VA = COL_KA + RET_QK_W
COL_GA = COL_VA + RET_V_W
COL_QB = COL_GA + RET_V_W
COL_KB = COL_QB + MOBA_W
COL_VB = COL_KB + MOBA_W
COL_GATE_A = COL_VB + MOBA_W
COL_GATE_B = COL_GATE_A + D_MODEL
D_IN = COL_GATE_B + D_MODEL

LANES = 128
SAMPLE_ROWS = 16
FLASH_STEPS = 1
SAMPLE_ATTN_ROWS = 8
SAMPLE_PAGES_PER_STEP = 16
MASKED = -1e9
LOG2_E = math.log2(math.e)
VMEM_LIMIT = 48 * 1024 * 1024
IN_PROJ_TILE = (1024, 1024)
MERGE_TILE = (1024, 512)
OUT_PROJ_TILE = (1024, 1024)
FFN_TILE = (512, 1024)

LOG_GAMMA = tuple(math.log(1.0 - 2.0 ** (-5.0 - h)) for h in range(RET_HEADS))

NT_DIMS = (((1,), (1,)), ((), ()))
TN_DIMS = (((0,), (0,)), ((), ()))


def _params(*sem):
    return pltpu.CompilerParams(dimension_semantics=sem, vmem_limit_bytes=VMEM_LIMIT)


def _tile(n, pref):
    t = min(n, pref)
    while n % t:
        t -= 1
    return t


def _rms(x, w):
    return x * lax.rsqrt(jnp.mean(x * x, axis=-1, keepdims=True) + NORM_EPS) * w


def _in_proj_kernel(x_ref, nw_ref, w_ref, o_ref, h_ref):
    @pl.when(pl.program_id(1) == 0)
    def _():
        h_ref[...] = _rms(x_ref[...], nw_ref[...]).astype(BF16)

    o_ref[...] = jnp.dot(h_ref[...], w_ref[...], preferred_element_type=F32)


def _layer_vec(w):
    return w.reshape(w.shape[0], 1, w.shape[1])


def in_proj(x, norm_w, w, layer):
    M, K = x.shape
    N = w.shape[2]
    tm, tn = _tile(M, IN_PROJ_TILE[0]), _tile(N, IN_PROJ_TILE[1])
    return pl.pallas_call(
        _in_proj_kernel,
        grid=(M // tm, N // tn),
        in_specs=[pl.BlockSpec((tm, K), lambda i, j: (i, 0)),
                  pl.BlockSpec((None, 1, K), lambda i, j: (layer, 0, 0)),
                  pl.BlockSpec((None, K, tn), lambda i, j: (layer, 0, j))],
        out_specs=pl.BlockSpec((tm, tn), lambda i, j: (i, j)),
        out_shape=jax.ShapeDtypeStruct((M, N), F32),
        scratch_shapes=[pltpu.VMEM((tm, K), BF16)],
        compiler_params=_params("arbitrary", "arbitrary"),
        name="in_proj",
    )(x, _layer_vec(norm_w), w)


def _merge_kernel(ya_ref, mo_ref, ga_ref, gb_ref, wpa_ref, wpb_ref, o_ref):
    ua = jnp.dot(ya_ref[...].astype(BF16), wpa_ref[...], preferred_element_type=F32)
    ub = jnp.dot(mo_ref[...].astype(BF16), wpb_ref[...], preferred_element_type=F32)
    o_ref[...] = (jax.nn.sigmoid(ga_ref[...]) * ua + jax.nn.sigmoid(gb_ref[...]) * ub).astype(BF16)


def merge_gates(y_a, moba_o, z, w_pa, w_pb, layer):
    M = y_a.shape[0]
    tm, tn = _tile(M, MERGE_TILE[0]), MERGE_TILE[1]
    ga0, gb0 = COL_GATE_A // tn, COL_GATE_B // tn
    return pl.pallas_call(
        _merge_kernel,
        grid=(M // tm, D_MODEL // tn),
        in_specs=[pl.BlockSpec((tm, RET_V_W), lambda i, j: (i, 0)),
                  pl.BlockSpec((tm, MOBA_W), lambda i, j: (i, 0)),
                  pl.BlockSpec((tm, tn), lambda i, j: (i, ga0 + j)),
                  pl.BlockSpec((tm, tn), lambda i, j: (i, gb0 + j)),
                  pl.BlockSpec((None, RET_V_W, tn), lambda i, j: (layer, 0, j)),
                  pl.BlockSpec((None, MOBA_W, tn), lambda i, j: (layer, 0, j))],
        out_specs=pl.BlockSpec((tm, tn), lambda i, j: (i, j)),
        out_shape=jax.ShapeDtypeStruct((M, D_MODEL), BF16),
        compiler_params=_params("arbitrary", "arbitrary"),
        name="merge_gates",
    )(y_a, moba_o, z, z, w_pa, w_pb)


def _out_proj_kernel(m_ref, w_ref, x_ref, o_ref):
    o_ref[...] = x_ref[...] + jnp.dot(m_ref[...], w_ref[...], preferred_element_type=F32)


def out_proj(m, w_o, x, layer):
    M, K = m.shape
    tm, tn = _tile(M, OUT_PROJ_TILE[0]), OUT_PROJ_TILE[1]
    return pl.pallas_call(
        _out_proj_kernel,
        grid=(M // tm, D_MODEL // tn),
        in_specs=[pl.BlockSpec((tm, K), lambda i, j: (i, 0)),
                  pl.BlockSpec((None, K, tn), lambda i, j: (layer, 0, j)),
                  pl.BlockSpec((tm, tn), lambda i, j: (i, j))],
        out_specs=pl.BlockSpec((tm, tn), lambda i, j: (i, j)),
        out_shape=jax.ShapeDtypeStruct((M, D_MODEL), F32),
        compiler_params=_params("arbitrary", "arbitrary"),
        name="out_proj",
    )(m, w_o, x)


def _ffn_kernel(x_ref, nw_ref, w1_ref, w2_ref, fw_ref, o_ref, h_ref, *, final):
    f = pl.program_id(1)

    @pl.when(f == 0)
    def _():
        x = x_ref[...]
        h_ref[...] = _rms(x, nw_ref[...]).astype(BF16)
        o_ref[...] = x

    a = jnp.dot(h_ref[...], w1_ref[...], preferred_element_type=F32)
    a = jnp.square(jnp.maximum(a, 0.0)).astype(BF16)
    o_ref[...] += jnp.dot(a, w2_ref[...], preferred_element_type=F32)

    if final:
        @pl.when(f == pl.num_programs(1) - 1)
        def _():
            o_ref[...] = _rms(o_ref[...], fw_ref[...])


def ffn(x, norm_w, w1, w2, layer, final_w, final):
    M, D = x.shape
    FF = w1.shape[2]
    tm, tf = _tile(M, FFN_TILE[0]), _tile(FF, FFN_TILE[1])
    return pl.pallas_call(
        functools.partial(_ffn_kernel, final=final),
        grid=(M // tm, FF // tf),
        in_specs=[pl.BlockSpec((tm, D), lambda i, f: (i, 0)),
                  pl.BlockSpec((None, 1, D), lambda i, f: (layer, 0, 0)),
                  pl.BlockSpec((None, D, tf), lambda i, f: (layer, 0, f)),
                  pl.BlockSpec((None, tf, D), lambda i, f: (layer, f, 0)),
                  pl.BlockSpec((1, D), lambda i, f: (0, 0))],
        out_specs=pl.BlockSpec((tm, D), lambda i, f: (i, 0)),
        out_shape=jax.ShapeDtypeStruct((M, D), F32),
        scratch_shapes=[pltpu.VMEM((tm, D), BF16)],
        compiler_params=_params("arbitrary", "arbitrary"),
        name="ffn",
    )(x, _layer_vec(norm_w), w1, w2, final_w.reshape(1, D))


def _rope_tables(pos):
    half = RET_DK // 2
    inv = ROPE_BASE ** (-np.arange(half, dtype=np.float64) / half)
    ang = np.asarray(pos, np.float64)[:, None] * inv[None, :]
    cos = np.concatenate([np.cos(ang), np.cos(ang)], axis=1)
    sin = np.concatenate([-np.sin(ang), np.sin(ang)], axis=1)
    return jnp.asarray(cos, F32), jnp.asarray(sin, F32)


def _retention_kernel(q_ref, k_ref, v_ref, g_ref, cos_ref, sin_ref, nw_ref, s0_ref, y_ref, s_ref, dm_ref,
                      *, c_rows, c_true):
    c = pl.program_id(1)

    @pl.when(c == 0)
    def _():
        s_ref[...] = s0_ref[...]

    @pl.when((pl.program_id(0) == 0) & (c == 0))
    def _():
        i = lax.broadcasted_iota(jnp.int32, (c_rows, c_rows), 0)
        j = lax.broadcasted_iota(jnp.int32, (c_rows, c_rows), 1)
        diff = (i - j).astype(F32)
        for h in range(RET_HEADS):
            dm_ref[h] = jnp.where(diff >= 0, jnp.exp(LOG_GAMMA[h] * jnp.maximum(diff, 0.0)), 0.0)

    cos = cos_ref[...]
    sin = sin_ref[...]
    row = lax.broadcasted_iota(jnp.int32, (c_rows, 1), 0).astype(F32)
    for h in range(RET_HEADS):
        lg = LOG_GAMMA[h]
        q = q_ref[0, :, h * RET_DK:(h + 1) * RET_DK]
        k = k_ref[0, :, h * RET_DK:(h + 1) * RET_DK]
        v = v_ref[0, :, h * RET_DV:(h + 1) * RET_DV].astype(BF16)
        q = q * cos + pltpu.roll(q, RET_DK // 2, 1) * sin
        k = (k * cos + pltpu.roll(k, RET_DK // 2, 1) * sin) * (RET_DK ** -0.5)
        qb = q.astype(BF16)
        scores = lax.dot_general(qb, k.astype(BF16), NT_DIMS, preferred_element_type=F32) * dm_ref[h]
        o = jnp.dot(scores.astype(BF16), v, preferred_element_type=F32)
        s = s_ref[0, h]
        q_dec = jnp.exp(lg * (row + 1.0))
        o = o + jnp.dot(qb, s.astype(BF16), preferred_element_type=F32) * q_dec
        k_dec = jnp.exp(lg * (c_true - 1.0 - row))
        kd = k * k_dec
        if c_true < c_rows:
            kd = jnp.where(row < c_true, kd, 0.0)
        s_ref[0, h] = math.exp(lg * c_true) * s + lax.dot_general(
            kd.astype(BF16), v, TN_DIMS, preferred_element_type=F32)
        y = _rms(o, nw_ref[:, h * RET_DV:(h + 1) * RET_DV])
        g = g_ref[0, :, h * RET_DV:(h + 1) * RET_DV]
        y_ref[0, :, h * RET_DV:(h + 1) * RET_DV] = (y * (g * jax.nn.sigmoid(g))).astype(BF16)


def retention(z3, cos, sin, ret_norm_w, layer, s0, s0_layer, c_rows, c_true):
    N, T, _ = z3.shape
    nc = T // c_rows
    kern = functools.partial(_retention_kernel, c_rows=c_rows, c_true=c_true)
    return pl.pallas_call(
        kern,
        grid=(N, nc),
        in_specs=[pl.BlockSpec((1, c_rows, RET_QK_W), lambda n, c: (n, c, COL_QA // RET_QK_W)),
                  pl.BlockSpec((1, c_rows, RET_QK_W), lambda n, c: (n, c, COL_KA // RET_QK_W)),
                  pl.BlockSpec((1, c_rows, RET_V_W), lambda n, c: (n, c, COL_VA // RET_V_W)),
                  pl.BlockSpec((1, c_rows, RET_V_W), lambda n, c: (n, c, COL_GA // RET_V_W)),
                  pl.BlockSpec((c_rows, RET_DK), lambda n, c: (c, 0)),
                  pl.BlockSpec((c_rows, RET_DK), lambda n, c: (c, 0)),
                  pl.BlockSpec((None, 1, RET_V_W), lambda n, c: (layer, 0, 0)),
                  pl.BlockSpec((None, 1, RET_HEADS, RET_DK, RET_DV), lambda n, c: (s0_layer, n, 0, 0, 0))],
        out_specs=[pl.BlockSpec((1, c_rows, RET_V_W), lambda n, c: (n, c, 0)),
                   pl.BlockSpec((1, RET_HEADS, RET_DK, RET_DV), lambda n, c: (n, 0, 0, 0))],
        out_shape=[jax.ShapeDtypeStruct((N, T, RET_V_W), BF16),
                   jax.ShapeDtypeStruct((N, RET_HEADS, RET_DK, RET_DV), F32)],
        scratch_shapes=[pltpu.VMEM((RET_HEADS, c_rows, c_rows), F32)],
        compiler_params=_params("arbitrary", "arbitrary"),
        name="retention",
    )(z3, z3, z3, z3, cos, sin, _layer_vec(ret_norm_w), s0)


def _t5_bucket_np(dist):
    n = np.maximum(dist, 0)
    max_exact = REL_BUCKETS // 2
    nf = np.maximum(n, 1).astype(np.float64)
    large = max_exact + (np.log(nf / max_exact) / math.log(REL_MAX_DIST / max_exact)
                         * (REL_BUCKETS - max_exact)).astype(np.int64)
    large = np.minimum(large, REL_BUCKETS - 1)
    return np.where(n < max_exact, n, large).astype(np.int32)


def _t5_bucket(dist):
    n = jnp.maximum(dist, 0)
    max_exact = REL_BUCKETS // 2
    nf = jnp.maximum(n, 1).astype(F32)
    large = max_exact + (jnp.log(nf / max_exact) / math.log(REL_MAX_DIST / max_exact)
                         * (REL_BUCKETS - max_exact)).astype(jnp.int32)
    large = jnp.minimum(large, REL_BUCKETS - 1)
    return jnp.where(n < max_exact, n, large)


def _bias_lookup(bucket, rb_ref, h, shift):
    out = jnp.zeros(bucket.shape, F32)
    for b in range(REL_BUCKETS):
        out = jnp.where(bucket == b, rb_ref[b, h] - shift, out)
    return out


def _top_blocks(gate, blk, eligible, axis=1):
    none = LANES
    g = jnp.where(eligible, gate, -jnp.inf)
    picks = []
    for _ in range(MOBA_TOPK):
        m = jnp.max(g, axis=axis, keepdims=True)
        cand = (g == m) & (m > -jnp.inf)
        idx = jnp.min(jnp.where(cand, blk, none), axis=axis, keepdims=True)
        idx = jnp.where(idx == none, -1, idx)
        picks.append(idx)
        g = jnp.where(blk == idx, -jnp.inf, g)
    return picks


def _moba_bias_kernel(rb_ref, bk_ref, o_ref):
    h = pl.program_id(0)
    far = rb_ref[REL_BUCKETS - 1, h]
    key = lax.broadcasted_iota(jnp.int32, (MOBA_BLOCK, MOBA_BLOCK), 0)
    query = lax.broadcasted_iota(jnp.int32, (MOBA_BLOCK, MOBA_BLOCK), 1)
    o_ref[0, 0] = jnp.where(query >= key, _bias_lookup(bk_ref[0], rb_ref, h, far) * LOG2_E, MASKED)
    o_ref[0, 1] = _bias_lookup(bk_ref[1], rb_ref, h, far) * LOG2_E


def moba_bias_tables(rel_bias):
    d = np.arange(MOBA_BLOCK)[None, :] - np.arange(MOBA_BLOCK)[:, None]
    buckets = jnp.asarray(np.stack([_t5_bucket_np(d), _t5_bucket_np(d + MOBA_BLOCK)]))
    return pl.pallas_call(
        _moba_bias_kernel,
        grid=(MOBA_HEADS,),
        in_specs=[pl.BlockSpec(memory_space=pltpu.SMEM),
                  pl.BlockSpec((2, MOBA_BLOCK, MOBA_BLOCK), lambda h: (0, 0, 0))],
        out_specs=pl.BlockSpec((1, 2, MOBA_BLOCK, MOBA_BLOCK), lambda h: (h, 0, 0, 0)),
        out_shape=jax.ShapeDtypeStruct((MOBA_HEADS, 2, MOBA_BLOCK, MOBA_BLOCK), F32),
        compiler_params=_params("arbitrary"),
        name="moba_bias_tables",
    )(rel_bias, buckets)


def _moba_select_kernel(q_ref, k_ref, v_ref, oh_ref, qa_ref, ka_ref, vb_ref, *, nb):
    q = q_ref[0]
    k = k_ref[0]
    S = q.shape[0]
    nbp = -(-nb // 8) * 8
    assert nbp <= LANES
    kmean = jnp.mean(k.reshape(nb, MOBA_BLOCK, MOBA_DH), axis=1)
    if nb < nbp:
        kmean = jnp.concatenate([kmean, jnp.zeros((nbp - nb, MOBA_DH), F32)], axis=0)
    gate = lax.dot_general(kmean, q, NT_DIMS, precision=lax.Precision.HIGHEST, preferred_element_type=F32)
    blk = lax.broadcasted_iota(jnp.int32, (nbp, S), 0)
    own = lax.shift_right_logical(lax.broadcasted_iota(jnp.int32, (nbp, S), 1), int(math.log2(MOBA_BLOCK)))
    blk_f = blk.astype(F32)
    picks = _top_blocks(gate, blk_f, blk < own, axis=0)
    allowed = blk == own
    for idx in picks:
        allowed = allowed | (blk_f == idx)
    eye = jnp.where(lax.broadcasted_iota(jnp.int32, (nbp, LANES), 0)
                    == lax.broadcasted_iota(jnp.int32, (nbp, LANES), 1), 1.0, 0.0).astype(BF16)
    allowed_q = lax.dot_general(jnp.where(allowed, 1.0, 0.0).astype(BF16), eye, TN_DIMS,
                                preferred_element_type=F32)
    qa_ref[0, 0, :, :MOBA_DH] = (q * (MOBA_DH ** -0.5 * LOG2_E)).astype(BF16)
    qa_ref[0, 0, :, MOBA_DH:] = ((1.0 - allowed_q) * MASKED).astype(BF16)
    ka_ref[0, 0, :, :MOBA_DH] = k.astype(BF16)
    ka_ref[0, 0, :, MOBA_DH:] = oh_ref[...]
    vb_ref[0, 0, :, :MOBA_DH] = v_ref[0].astype(BF16)
    vb_ref[0, 0, :, MOBA_DH:] = jnp.where(lax.broadcasted_iota(jnp.int32, (S, MOBA_DH), 1) == 0, 1.0, 0.0).astype(BF16)


def moba_select(z3):
    B, S, _ = z3.shape
    nb = S // MOBA_BLOCK
    kern = functools.partial(_moba_select_kernel, nb=nb)
    onehot = jnp.asarray(np.arange(S)[:, None] // MOBA_BLOCK == np.arange(LANES)[None, :], BF16)
    return pl.pallas_call(
        kern,
        grid=(B, MOBA_HEADS),
        in_specs=[pl.BlockSpec((1, S, MOBA_DH), lambda b, h: (b, 0, COL_QB // MOBA_DH + h)),
                  pl.BlockSpec((1, S, MOBA_DH), lambda b, h: (b, 0, COL_KB // MOBA_DH + h)),
                  pl.BlockSpec((1, S, MOBA_DH), lambda b, h: (b, 0, COL_VB // MOBA_DH + h)),
                  pl.BlockSpec((S, LANES), lambda b, h: (0, 0))],
        out_specs=[pl.BlockSpec((1, 1, S, 2 * MOBA_DH), lambda b, h: (b, h, 0, 0)),
                   pl.BlockSpec((1, 1, S, 2 * MOBA_DH), lambda b, h: (b, h, 0, 0)),
                   pl.BlockSpec((1, 1, S, 2 * MOBA_DH), lambda b, h: (b, h, 0, 0))],
        out_shape=[jax.ShapeDtypeStruct((B, MOBA_HEADS, S, 2 * MOBA_DH), BF16)] * 3,
        compiler_params=_params("arbitrary", "arbitrary"),
        name="moba_select",
    )(z3, z3, z3, onehot)


def _moba_flash_tile(qi, q_ref, k_ref, v_ref, bias_ref, o_ref, s_ref, p_ref):
    q = q_ref[0, 0, qi * MOBA_BLOCK:(qi + 1) * MOBA_BLOCK, :]
    n = (qi + 1) * MOBA_BLOCK
    s_ref[:n, :] = lax.dot_general(k_ref[0, 0, :n, :], q, NT_DIMS, preferred_element_type=F32)
    m8 = jnp.full((8, MOBA_BLOCK), -jnp.inf, F32)
    for j in range(qi + 1):
        rows = slice(j * MOBA_BLOCK, (j + 1) * MOBA_BLOCK)
        s = s_ref[rows, :]
        if j >= qi - 1:
            s = s + bias_ref[0, qi - j]
            s_ref[rows, :] = s
        m8 = jnp.maximum(m8, jnp.max(s.reshape(MOBA_BLOCK // 8, 8, MOBA_BLOCK), axis=0))
    m = jnp.max(m8, axis=0, keepdims=True)
    for j in range(qi + 1):
        rows = slice(j * MOBA_BLOCK, (j + 1) * MOBA_BLOCK)
        p_ref[rows, :] = jnp.exp2(s_ref[rows, :] - m).astype(BF16)
    acc = lax.dot_general(p_ref[:n, :], v_ref[0, 0, :n, :], TN_DIMS, preferred_element_type=F32)
    o_ref[0, qi * MOBA_BLOCK:(qi + 1) * MOBA_BLOCK, :] = (
        acc[:, :MOBA_DH] / acc[:, MOBA_DH:MOBA_DH + 1]).astype(BF16)


def _flash_groups(nb, steps):
    pairs = [(i, nb - 1 - i) for i in range(nb // 2)]
    return [sum(pairs[g::steps], ()) for g in range(steps)]


def _moba_flash_kernel(q_ref, k_ref, v_ref, bias_ref, o_ref, s_ref, p_ref, *, groups):
    i = pl.program_id(2)
    for g, blocks in enumerate(groups):
        @pl.when(i == g)
        def _(blocks=blocks):
            for qi in blocks:
                _moba_flash_tile(qi, q_ref, k_ref, v_ref, bias_ref, o_ref, s_ref, p_ref)


def moba_flash(q_aug, k_aug, v, bias):
    B, H, S, _ = q_aug.shape
    nb = S // MOBA_BLOCK
    steps = FLASH_STEPS if nb % (2 * FLASH_STEPS) == 0 else 1
    groups = _flash_groups(nb, steps) if nb % 2 == 0 else [tuple(range(nb))]
    return pl.pallas_call(
        functools.partial(_moba_flash_kernel, groups=groups),
        grid=(B, H, len(groups)),
        in_specs=[pl.BlockSpec((1, 1, S, 2 * MOBA_DH), lambda b, h, i: (b, h, 0, 0)),
                  pl.BlockSpec((1, 1, S, 2 * MOBA_DH), lambda b, h, i: (b, h, 0, 0)),
                  pl.BlockSpec((1, 1, S, 2 * MOBA_DH), lambda b, h, i: (b, h, 0, 0)),
                  pl.BlockSpec((1, 2, MOBA_BLOCK, MOBA_BLOCK), lambda b, h, i: (h, 0, 0, 0))],
        out_specs=pl.BlockSpec((1, S, MOBA_DH), lambda b, h, i: (b, 0, h)),
        out_shape=jax.ShapeDtypeStruct((B, S, MOBA_W), BF16),
        scratch_shapes=[pltpu.VMEM((S, MOBA_BLOCK), F32), pltpu.VMEM((S, MOBA_BLOCK), BF16)],
        compiler_params=_params("arbitrary", "arbitrary", "arbitrary"),
        name="moba_flash",
    )(q_aug, k_aug, v, bias)


def _sample_rows(ref, rows):
    return jnp.concatenate([ref[0, :rows, h * MOBA_DH:(h + 1) * MOBA_DH] for h in range(MOBA_HEADS)], axis=0)


def _sample_shifted_bias(dist, rb_ref, rows):
    bucket = _t5_bucket(dist)
    return jnp.concatenate(
        [_bias_lookup(bucket[h * rows:(h + 1) * rows], rb_ref, h, rb_ref[REL_BUCKETS - 1, h])
         for h in range(MOBA_HEADS)], axis=0)


def _sample_blocks_kernel(pt_ref, q_ref, rb_ref, *rest, past_len, page_size, n_pages, pages_per_step):
    pg = pages_per_step
    k_refs, v_refs = rest[:pg], rest[pg:2 * pg]
    acc_ref, m_ref, l_ref, g_ref, qa_ref, qf_ref, hm_ref, s_ref = rest[2 * pg:]
    j = pl.program_id(1)
    rows = SAMPLE_ATTN_ROWS
    ra = MOBA_HEADS * rows
    cols = page_size * MOBA_HEADS
    row_bits, head_bits = int(math.log2(rows)), int(math.log2(MOBA_HEADS))
    ppb = MOBA_BLOCK // page_size
    bps = pg // ppb

    @pl.when(j == 0)
    def _():
        qf = _sample_rows(q_ref, rows) * (MOBA_DH ** -0.5)
        qf_ref[...] = qf
        qa_ref[...] = qf.astype(BF16)
        r = lax.broadcasted_iota(jnp.int32, (ra, cols), 0)
        c = lax.broadcasted_iota(jnp.int32, (ra, cols), 1)
        hm_ref[...] = jnp.where(lax.shift_right_logical(r, row_bits) == (c & (MOBA_HEADS - 1)), 0.0, MASKED)
        m_ref[...] = jnp.zeros(m_ref.shape, F32)
        l_ref[...] = jnp.zeros(l_ref.shape, F32)
        g_ref[...] = jnp.zeros(g_ref.shape, F32)

    qa = qa_ref[...]
    qf = qf_ref[...]
    lane = lax.broadcasted_iota(jnp.int32, (ra, LANES), 1)
    m_all, l_all, g_all = m_ref[0], l_ref[0], g_ref[0]
    n_near = -(-(REL_MAX_DIST - 1) // page_size)
    assert n_near <= ppb
    for bb in range(bps):
        m_wide = jnp.full((ra, LANES), -jnp.inf, F32)
        ksum = jnp.zeros((MOBA_HEADS, MOBA_DH), F32)
        for p in range(ppb):
            r = bb * ppb + p
            k = k_refs[r][0]
            ksum = ksum + jnp.sum(k.reshape(page_size, MOBA_HEADS, MOBA_DH), axis=0)
            s = lax.dot_general(qa, k.astype(BF16), NT_DIMS, preferred_element_type=F32)
            s_ref[p] = s + hm_ref[...]
            if r >= pg - n_near:
                @pl.when(j == pl.num_programs(1) - 1)
                def _(r=r, p=p):
                    rr = lax.broadcasted_iota(jnp.int32, (ra, cols), 0)
                    cc = lax.broadcasted_iota(jnp.int32, (ra, cols), 1)
                    kpos = (n_pages - pg + r) * page_size + lax.shift_right_logical(cc, head_bits)
                    s_ref[p] = s_ref[p] + _sample_shifted_bias(past_len + (rr & (rows - 1)) - kpos, rb_ref, rows)
            s = s_ref[p]
            for c0 in range(0, cols, LANES):
                m_wide = jnp.maximum(m_wide, s[:, c0:c0 + LANES])
        m_b = jnp.max(m_wide, axis=1, keepdims=True)
        mb = jnp.broadcast_to(m_b, (ra, cols))
        l_wide = jnp.zeros((ra, LANES), F32)
        acc = jnp.zeros((ra, MOBA_DH), F32)
        for p in range(ppb):
            e = jnp.exp(s_ref[p] - mb)
            for c0 in range(0, cols, LANES):
                l_wide = l_wide + e[:, c0:c0 + LANES]
            acc = acc + jnp.dot(e.astype(BF16), v_refs[bb * ppb + p][0].astype(BF16), preferred_element_type=F32)
        acc_ref[0, bb] = acc
        kmean = ksum * (1.0 / MOBA_BLOCK)
        kmean_rows = jnp.concatenate(
            [jnp.broadcast_to(kmean[h:h + 1, :], (rows, MOBA_DH)) for h in range(MOBA_HEADS)], axis=0)
        gate = jnp.sum(qf * kmean_rows, axis=1, keepdims=True)
        here = lane == j * bps + bb
        m_all = jnp.where(here, m_b, m_all)
        l_all = jnp.where(here, jnp.sum(l_wide, axis=1, keepdims=True), l_all)
        g_all = jnp.where(here, gate, g_all)
    m_ref[0], l_ref[0], g_ref[0] = m_all, l_all, g_all


def sample_blocks(page_table, z3, cache_k, cache_v, layer, rel_bias, past_len):
    Ns, R, _ = z3.shape
    n_pages = page_table.shape[1]
    page_size = cache_k.shape[2] // MOBA_HEADS
    cols = page_size * MOBA_HEADS
    ra = MOBA_HEADS * SAMPLE_ATTN_ROWS
    pg = SAMPLE_PAGES_PER_STEP
    ppb = MOBA_BLOCK // page_size
    nb = n_pages // ppb
    assert n_pages % pg == 0 and pg % ppb == 0 and past_len == n_pages * page_size and nb <= LANES
    kern = functools.partial(_sample_blocks_kernel, past_len=past_len, page_size=page_size, n_pages=n_pages,
                             pages_per_step=pg)

    def page_spec(r):
        return pl.BlockSpec((None, 1, cols, MOBA_DH),
                            lambda n, j, pt: (layer, pt[n * n_pages + j * pg + r], 0, 0))

    stat_spec = pl.BlockSpec((1, ra, LANES), lambda n, j, pt: (n, 0, 0))
    stat_shape = jax.ShapeDtypeStruct((Ns, ra, LANES), F32)
    grid_spec = pltpu.PrefetchScalarGridSpec(
        num_scalar_prefetch=1,
        grid=(Ns, n_pages // pg),
        in_specs=([pl.BlockSpec((1, R, MOBA_W), lambda n, j, pt: (n, 0, COL_QB // MOBA_W)),
                   pl.BlockSpec(memory_space=pltpu.SMEM)]
                  + [page_spec(r) for r in range(pg)] + [page_spec(r) for r in range(pg)]),
        out_specs=[pl.BlockSpec((1, pg // ppb, ra, MOBA_DH), lambda n, j, pt: (n, j, 0, 0)),
                   stat_spec, stat_spec, stat_spec],
        scratch_shapes=[pltpu.VMEM((ra, MOBA_DH), BF16), pltpu.VMEM((ra, MOBA_DH), F32),
                        pltpu.VMEM((ra, cols), F32), pltpu.VMEM((ppb, ra, cols), F32)],
    )
    return pl.pallas_call(
        kern,
        grid_spec=grid_spec,
        out_shape=[jax.ShapeDtypeStruct((Ns, nb, ra, MOBA_DH), F32), stat_shape, stat_shape, stat_shape],
        compiler_params=_params("arbitrary", "arbitrary"),
        name="sample_blocks",
    )(page_table.reshape(-1), z3, rel_bias, *([cache_k] * pg), *([cache_v] * pg))


def _sample_combine_kernel(q_ref, kn_ref, vn_ref, acc_ref, m_ref, l_ref, g_ref, rb_ref, o_ref,
                           *, t_true, past_len, nb_past):
    rows = SAMPLE_ATTN_ROWS
    ra = MOBA_HEADS * rows
    row_bits = int(math.log2(rows))
    qa = (_sample_rows(q_ref, rows) * (MOBA_DH ** -0.5)).astype(BF16)
    kn = _sample_rows(kn_ref, rows).astype(BF16)
    vn = _sample_rows(vn_ref, rows).astype(BF16)
    s = lax.dot_general(qa, kn, NT_DIMS, preferred_element_type=F32)
    r = lax.broadcasted_iota(jnp.int32, (ra, ra), 0)
    c = lax.broadcasted_iota(jnp.int32, (ra, ra), 1)
    t, kt = r & (rows - 1), c & (rows - 1)
    valid = ((lax.shift_right_logical(r, row_bits) == lax.shift_right_logical(c, row_bits))
             & (kt <= t) & (kt < t_true))
    s = jnp.where(valid, s + _sample_shifted_bias(t - kt, rb_ref, rows), MASKED)
    m_own = jnp.max(s, axis=1, keepdims=True)
    e = jnp.exp(s - m_own)
    l_own = jnp.sum(e, axis=1, keepdims=True)
    acc_own = jnp.dot(e.astype(BF16), vn, preferred_element_type=F32)

    lane = lax.broadcasted_iota(jnp.int32, (ra, LANES), 1)
    pos = past_len + (lax.broadcasted_iota(jnp.int32, (ra, LANES), 0) & (rows - 1))
    own = lax.shift_right_logical(pos, int(math.log2(MOBA_BLOCK)))
    picks = _top_blocks(g_ref[0], lane, (lane < own) & (lane < nb_past))
    chosen = lane == picks[0]
    for idx in picks[1:]:
        chosen = chosen | (lane == idx)
    m_blk = jnp.where(chosen, m_ref[0], -jnp.inf)
    m_tot = jnp.maximum(m_own, jnp.max(m_blk, axis=1, keepdims=True))
    w_blk = jnp.exp(m_blk - m_tot)
    w_own = jnp.exp(m_own - m_tot)
    l_tot = w_own * l_own + jnp.sum(w_blk * l_ref[0], axis=1, keepdims=True)
    acc = w_own * acc_own
    for b in range(nb_past):
        acc = acc + w_blk[:, b:b + 1] * acc_ref[0, b]
    out = acc / l_tot
    o_ref[...] = jnp.zeros(o_ref.shape, F32)
    for h in range(MOBA_HEADS):
        o_ref[0, :rows, h * MOBA_DH:(h + 1) * MOBA_DH] = out[h * rows:(h + 1) * rows]


def sample_combine(z3, acc, m, l, g, rel_bias, t_true, past_len):
    Ns, R, _ = z3.shape
    nb_past, ra = acc.shape[1], acc.shape[2]
    assert t_true <= SAMPLE_ATTN_ROWS <= R
    kern = functools.partial(_sample_combine_kernel, t_true=t_true, past_len=past_len, nb_past=nb_past)

    def z_spec(col):
        return pl.BlockSpec((1, R, MOBA_W), lambda n: (n, 0, col // MOBA_W))

    stat_spec = pl.BlockSpec((1, ra, LANES), lambda n: (n, 0, 0))
    return pl.pallas_call(
        kern,
        grid=(Ns,),
        in_specs=[z_spec(COL_QB), z_spec(COL_KB), z_spec(COL_VB),
                  pl.BlockSpec((1, nb_past, ra, MOBA_DH), lambda n: (n, 0, 0, 0)),
                  stat_spec, stat_spec, stat_spec,
                  pl.BlockSpec(memory_space=pltpu.SMEM)],
        out_specs=pl.BlockSpec((1, R, MOBA_W), lambda n: (n, 0, 0)),
        out_shape=jax.ShapeDtypeStruct((Ns, R, MOBA_W), F32),
        compiler_params=_params("arbitrary"),
        name="sample_combine",
    )(z3, z3, z3, acc, m, l, g, rel_bias)


def _layer_tail(x, z, y_a, moba_o, w, layer):
    m = merge_gates(y_a, moba_o, z, w["w_pa"], w["w_pb"], layer)
    x = out_proj(m, w["w_o"], x, layer)
    last = layer == w["w_o"].shape[0] - 1
    return ffn(x, w["norm2_w"], w["w_ff1"], w["w_ff2"], layer, w["final_norm_w"], last)


def kernel(x_prompt, x_sample, cache_k, cache_v, state_ret, page_table, rel_bias, norm1_w, w_in, ret_norm_w,
           w_pa, w_pb, w_o, norm2_w, w_ff1, w_ff2, final_norm_w):
    depth = w_in.shape[0]
    B, S, D = x_prompt.shape
    Ns, Ts, _ = x_sample.shape
    n_phys, page_size = cache_k.shape[1], cache_k.shape[2]
    n_pages = page_table.shape[1]
    past_len = n_pages * page_size
    assert D == D_MODEL and S % MOBA_BLOCK == 0 and past_len % MOBA_BLOCK == 0 and Ts <= SAMPLE_ROWS

    w = dict(w_pa=w_pa.astype(BF16), w_pb=w_pb.astype(BF16), w_o=w_o.astype(BF16), norm2_w=norm2_w,
             w_ff1=w_ff1.astype(BF16), w_ff2=w_ff2.astype(BF16), final_norm_w=final_norm_w)
    w_in = w_in.astype(BF16)

    ret_chunk = _tile(S, 256)
    cos_p, sin_p = _rope_tables(np.arange(S))
    bias_tables = moba_bias_tables(rel_bias)
    zero_state = jnp.zeros((1, B, RET_HEADS, RET_DK, RET_DV), F32)
    xp = x_prompt.reshape(B * S, D)
    pk, pv, ps = [], [], []
    for l in range(depth):
        z = in_proj(xp, norm1_w, w_in, l)
        z3 = z.reshape(B, S, D_IN)
        y_a, s_fin = retention(z3, cos_p, sin_p, ret_norm_w, l, zero_state, 0, ret_chunk, ret_chunk)
        q_aug, k_aug, v_b = moba_select(z3)
        moba_o = moba_flash(q_aug, k_aug, v_b, bias_tables)
        xp = _layer_tail(xp, z, y_a.reshape(B * S, RET_V_W), moba_o.reshape(B * S, MOBA_W), w, l)
        pk.append(z3[:, :, COL_KB:COL_KB + MOBA_W].reshape(B, S, MOBA_HEADS, MOBA_DH))
        pv.append(z3[:, :, COL_VB:COL_VB + MOBA_W].reshape(B, S, MOBA_HEADS, MOBA_DH))
        ps.append(s_fin)
    y_prompt = xp.reshape(B, S, D)

    R = SAMPLE_ROWS
    cos_s, sin_s = _rope_tables(past_len + np.arange(R))
    ck = cache_k.reshape(depth, n_phys, page_size * MOBA_HEADS, MOBA_DH)
    cv = cache_v.reshape(depth, n_phys, page_size * MOBA_HEADS, MOBA_DH)
    xs = jnp.zeros((Ns, R, D), F32).at[:, :Ts].set(x_sample).reshape(Ns * R, D)
    sk, sv, ss = [], [], []
    for l in range(depth):
        z = in_proj(xs, norm1_w, w_in, l)
        z3 = z.reshape(Ns, R, D_IN)
        y_a, s_new = retention(z3, cos_s, sin_s, ret_norm_w, l, state_ret, l, R, Ts)
        blk_acc, blk_m, blk_l, blk_g = sample_blocks(page_table, z3, ck, cv, l, rel_bias, past_len)
        moba_o = sample_combine(z3, blk_acc, blk_m, blk_l, blk_g, rel_bias, Ts, past_len)
        xs = _layer_tail(xs, z, y_a.reshape(Ns * R, RET_V_W), moba_o.reshape(Ns * R, MOBA_W), w, l)
        sk.append(z3[:, :Ts, COL_KB:COL_KB + MOBA_W].reshape(Ns, Ts, MOBA_HEADS, MOBA_DH))
        sv.append(z3[:, :Ts, COL_VB:COL_VB + MOBA_W].reshape(Ns, Ts, MOBA_HEADS, MOBA_DH))
        ss.append(s_new)
    y_sample = xs.reshape(Ns, R, D)[:, :Ts]

    return (y_prompt, y_sample, jnp.stack(pk), jnp.stack(pv), jnp.stack(ps),
            jnp.stack(sk), jnp.stack(sv), jnp.stack(ss))
```
